```python
import math
import jax, jax.numpy as jnp
from jax import lax
import numpy as np

D_MODEL = 1024
BATCH = 1
SEQ = 16384
DEPTH = 4

N_MIXERS = 2
N_SSD_LAYERS = (DEPTH + 1) // 2
N_SC_LAYERS = DEPTH // 2
RMS_EPS = 1e-5
D_FF = 2816
SSD_EXPAND = 2
SSD_D_INNER = SSD_EXPAND * D_MODEL
SSD_HEAD_DIM = 64
SSD_N_HEADS = SSD_D_INNER // SSD_HEAD_DIM
SSD_N_GROUPS = 4
SSD_HEADS_PER_GROUP = SSD_N_HEADS // SSD_N_GROUPS
SSD_D_STATE = 128
SSD_CONV_W = 4
SSD_CHUNK = 128
SSD_CONV_DIM = SSD_D_INNER + 2 * SSD_N_GROUPS * SSD_D_STATE
SSD_IN_DIM = SSD_D_INNER + SSD_CONV_DIM + SSD_N_HEADS
SSD_DT_MIN = 1e-3
SSD_DT_MAX = 1e-1
SC_CONV_W = 3

kernel_name = "hybrid_ssd_shortconv_macaron"


def rmsnorm(x, w):
    xf = x.astype(jnp.float32)
    inv = lax.rsqrt(jnp.mean(xf * xf, axis=-1, keepdims=True) + RMS_EPS)
    return (xf * inv).astype(x.dtype) * w


def swiglu(h, w_gate, w_up, w_down):
    return (jax.nn.silu(h @ w_gate) * (h @ w_up)) @ w_down


def causal_dwconv(u, w):
    k_w = w.shape[0]
    s = u.shape[1]
    upad = jnp.pad(u, ((0, 0), (k_w - 1, 0), (0, 0)))
    out = upad[:, 0:s] * w[0]
    for k in range(1, k_w):
        out = out + upad[:, k:k + s] * w[k]
    return out


def causal_decay(a_cs):
    t = a_cs.shape[-1]
    seg = a_cs[..., :, None] - a_cs[..., None, :]
    mask = jnp.tril(jnp.ones((t, t), dtype=bool))
    return jnp.exp(jnp.where(mask, seg, -jnp.inf))


def ssd_chunked(x, dt, a, bm, cm):
    b, s = x.shape[0], x.shape[1]
    nc = s // SSD_CHUNK
    G, E, P, L = SSD_N_GROUPS, SSD_HEADS_PER_GROUP, SSD_HEAD_DIM, SSD_CHUNK
    x_dt = x.astype(jnp.float32) * dt[..., None]
    xc = x_dt.reshape(b, nc, L, G, E, P)
    ac = (dt * a).reshape(b, nc, L, G, E).transpose(0, 3, 4, 1, 2)
    bc = bm.astype(jnp.float32).reshape(b, nc, L, G, SSD_D_STATE)
    cc = cm.astype(jnp.float32).reshape(b, nc, L, G, SSD_D_STATE)
    a_cs = jnp.cumsum(ac, axis=-1)
    lmat = causal_decay(a_cs)
    cb = jnp.einsum("bclgn,bcsgn->bgcls", cc, bc)
    y_diag = jnp.einsum("bgcls,bgecls,bcsgep->bclgep", cb, lmat, xc)
    decay_states = jnp.exp(a_cs[..., -1:] - a_cs)
    states = jnp.einsum("bclgn,bgecl,bclgep->bcgepn", bc, decay_states, xc)
    states = jnp.concatenate([jnp.zeros_like(states[:, :1]), states], axis=1)
    chunk_tot = jnp.pad(a_cs[..., -1], ((0, 0), (0, 0), (0, 0), (1, 0)))
    decay_chunk = causal_decay(jnp.cumsum(chunk_tot, axis=-1))
    new_states = jnp.einsum("bgezc,bcgepn->bzgepn", decay_chunk, states)
    prev_states = new_states[:, :-1]
    y_off = jnp.einsum("bclgn,bcgepn,bgecl->bclgep", cc, prev_states, jnp.exp(a_cs))
    return (y_diag + y_off).reshape(b, s, SSD_N_HEADS, P)


def ssd_mixer(h, w_in, conv_w, conv_b, dt_bias, a_log, d_skip, norm_w, w_out):
    b, s, _ = h.shape
    zxbcdt = h @ w_in
    z = zxbcdt[..., :SSD_D_INNER]
    xbc = zxbcdt[..., SSD_D_INNER:SSD_D_INNER + SSD_CONV_DIM]
    dt_raw = zxbcdt[..., SSD_D_INNER + SSD_CONV_DIM:]
    xbc = jax.nn.silu(causal_dwconv(xbc, conv_w) + conv_b)
    gn = SSD_N_GROUPS * SSD_D_STATE
    xs = xbc[..., :SSD_D_INNER].reshape(b, s, SSD_N_HEADS, SSD_HEAD_DIM)
    bm = xbc[..., SSD_D_INNER:SSD_D_INNER + gn].reshape(b, s, SSD_N_GROUPS, SSD_D_STATE)
    cm = xbc[..., SSD_D_INNER + gn:].reshape(b, s, SSD_N_GROUPS, SSD_D_STATE)
    dt = jax.nn.softplus(dt_raw.astype(jnp.float32) + dt_bias.astype(jnp.float32))
    a = -jnp.exp(a_log.astype(jnp.float32))
    y = ssd_chunked(xs, dt, a, bm, cm)
    y = y + xs.astype(jnp.float32) * d_skip.astype(jnp.float32)[:, None]
    y = y.reshape(b, s, SSD_D_INNER)
    g = y * jax.nn.silu(z.astype(jnp.float32))
    gg = g.reshape(b, s, SSD_N_GROUPS, SSD_D_INNER // SSD_N_GROUPS)
    gg = gg * lax.rsqrt(jnp.mean(gg * gg, axis=-1, keepdims=True) + RMS_EPS)
    g = gg.reshape(b, s, SSD_D_INNER).astype(h.dtype) * norm_w
    return g @ w_out


def shortconv_mixer(h, w_in, conv_w, w_out):
    bcu = h @ w_in
    bg = bcu[..., :D_MODEL]
    cg = bcu[..., D_MODEL:2 * D_MODEL]
    u = bcu[..., 2 * D_MODEL:]
    v = causal_dwconv(cg * u, conv_w)
    return (bg * v) @ w_out


def setup_inputs(seed: int = 0) -> dict:
    key = jax.random.key(seed)
    ks = jax.random.split(key, 20)
    f32 = jnp.float32
    nrm = lambda k, shape, fan_in: jax.random.normal(k, shape, f32) * (fan_in ** -0.5)
    x = jax.random.normal(ks[0], (BATCH, SEQ, D_MODEL), f32)
    norm_w = 1.0 + 0.01 * jax.random.normal(ks[1], (DEPTH, 3, D_MODEL), f32)
    ffn_w_gate = nrm(ks[2], (DEPTH, 2, D_MODEL, D_FF), D_MODEL)
    ffn_w_up = nrm(ks[3], (DEPTH, 2, D_MODEL, D_FF), D_MODEL)
    ffn_w_down = nrm(ks[4], (DEPTH, 2, D_FF, D_MODEL), D_FF)
    ssd_w_in = nrm(ks[5], (N_SSD_LAYERS, D_MODEL, SSD_IN_DIM), D_MODEL)
    ssd_conv_w = nrm(ks[6], (N_SSD_LAYERS, SSD_CONV_W, SSD_CONV_DIM), SSD_CONV_W)
    ssd_conv_b = 0.01 * jax.random.normal(ks[7], (N_SSD_LAYERS, SSD_CONV_DIM), f32)
    dt0 = jnp.exp(jax.random.uniform(ks[8], (N_SSD_LAYERS, SSD_N_HEADS), f32)
                  * (math.log(SSD_DT_MAX) - math.log(SSD_DT_MIN)) + math.log(SSD_DT_MIN))
    ssd_dt_bias = dt0 + jnp.log(-jnp.expm1(-dt0))
    ssd_a_log = jnp.log(jax.random.uniform(ks[9], (N_SSD_LAYERS, SSD_N_HEADS), f32, 1.0, 16.0))
    ssd_d = 1.0 + 0.1 * jax.random.normal(ks[10], (N_SSD_LAYERS, SSD_N_HEADS), f32)
    ssd_norm_w = 1.0 + 0.01 * jax.random.normal(ks[11], (N_SSD_LAYERS, SSD_D_INNER), f32)
    ssd_w_out = nrm(ks[12], (N_SSD_LAYERS, SSD_D_INNER, D_MODEL), SSD_D_INNER)
    sc_w_in = nrm(ks[13], (N_SC_LAYERS, D_MODEL, 3 * D_MODEL), D_MODEL)
    sc_conv_w = nrm(ks[14], (N_SC_LAYERS, SC_CONV_W, D_MODEL), SC_CONV_W)
    sc_w_out = nrm(ks[15], (N_SC_LAYERS, D_MODEL, D_MODEL), D_MODEL)
    final_norm_w = 1.0 + 0.01 * jax.random.normal(ks[16], (D_MODEL,), f32)
    return {"x": x, "norm_w": norm_w, "ffn_w_gate": ffn_w_gate, "ffn_w_up": ffn_w_up,
            "ffn_w_down": ffn_w_down, "ssd_w_in": ssd_w_in, "ssd_conv_w": ssd_conv_w,
            "ssd_conv_b": ssd_conv_b, "ssd_dt_bias": ssd_dt_bias, "ssd_a_log": ssd_a_log,
            "ssd_d": ssd_d, "ssd_norm_w": ssd_norm_w, "ssd_w_out": ssd_w_out,
            "sc_w_in": sc_w_in, "sc_conv_w": sc_conv_w, "sc_w_out": sc_w_out,
            "final_norm_w": final_norm_w}


def reference(x, norm_w, ffn_w_gate, ffn_w_up, ffn_w_down, ssd_w_in, ssd_conv_w, ssd_conv_b,
              ssd_dt_bias, ssd_a_log, ssd_d, ssd_norm_w, ssd_w_out, sc_w_in, sc_conv_w,
              sc_w_out, final_norm_w):
    for i in range(DEPTH):
        x = x + 0.5 * swiglu(rmsnorm(x, norm_w[i, 0]), ffn_w_gate[i, 0], ffn_w_up[i, 0], ffn_w_down[i, 0])
        h = rmsnorm(x, norm_w[i, 1])
        j = i // N_MIXERS
        if i % N_MIXERS == 0:
            mix = ssd_mixer(h, ssd_w_in[j], ssd_conv_w[j], ssd_conv_b[j], ssd_dt_bias[j],
                            ssd_a_log[j], ssd_d[j], ssd_norm_w[j], ssd_w_out[j])
        else:
            mix = shortconv_mixer(h, sc_w_in[j], sc_conv_w[j], sc_w_out[j])
        x = x + mix
        x = x + 0.5 * swiglu(rmsnorm(x, norm_w[i, 2]), ffn_w_gate[i, 1], ffn_w_up[i, 1], ffn_w_down[i, 1])
    return rmsnorm(x, final_norm_w)
```

```python
import functools
import math

import jax
import jax.numpy as jnp
from jax import lax
from jax.experimental import pallas as pl
from jax.experimental.pallas import tpu as pltpu

F32 = jnp.float32
BF16 = jnp.bfloat16

RMS_EPS = 1e-5
SSD_HEAD_DIM = 64
SSD_N_GROUPS = 4
SSD_D_STATE = 128
SSD_CHUNK = 128
N_MIXERS = 2

V7X_LANES = 128
V7X_SUBLANES = 8
V7X_VMEM_BYTES = 64 * 1024 * 1024

FFN_ROW_TILE = 512
FFN_COL_CHUNK = 256
SC_ROW_TILE = 512
SSD_ROW_TILE = 256
SSD_COL_CHUNK = 512


def _vmem_limit(resident_bytes):
    return int(min(2 * resident_bytes, V7X_VMEM_BYTES - 8 * 1024 * 1024))


def _nbytes(shape, dtype):
    return math.prod(shape) * jnp.dtype(dtype).itemsize


def _rmsnorm(x, w):
    inv = lax.rsqrt(jnp.mean(x * x, axis=-1, keepdims=True) + RMS_EPS)
    return (x * inv) * w


def _silu(x):
    return x * jax.nn.sigmoid(x)


def _softplus(x):
    return jnp.maximum(x, 0.0) + jnp.log1p(jnp.exp(-jnp.abs(x)))


def _dot(a, b):
    return jnp.dot(a, b, preferred_element_type=F32)


def _resident(shape):
    return pl.BlockSpec(shape, lambda i: (0,) * len(shape), pipeline_mode=pl.Buffered(1))


def _row_tile(tm, d):
    return pl.BlockSpec((tm, d), lambda i: (i, 0))


def _ffn_kernel(x_ref, nw_ref, wg_ref, wu_ref, wd_ref, fnw_ref, o_ref, act_ref, *, col_chunk, final_norm):
    x = x_ref[...]
    h = _rmsnorm(x, nw_ref[...]).astype(BF16)
    d_ff = wg_ref.shape[1]
    for c in range(d_ff // col_chunk):
        cols = slice(c * col_chunk, (c + 1) * col_chunk)
        g = _dot(h, wg_ref[:, cols])
        u = _dot(h, wu_ref[:, cols])
        act_ref[:, cols] = (_silu(g) * u).astype(BF16)
    out = x + 0.5 * _dot(act_ref[...], wd_ref[...])
    if final_norm:
        out = _rmsnorm(out, fnw_ref[...])
    o_ref[...] = out


def _ffn(x, nw, wg, wu, wd, fnw, *, final_norm):
    s, d = x.shape
    d_ff = wg.shape[1]
    tm = FFN_ROW_TILE
    assert s % tm == 0 and d_ff % FFN_COL_CHUNK == 0
    resident = (4 * _nbytes((tm, d), F32) + 3 * _nbytes((d, d_ff), BF16) + _nbytes((tm, d_ff), BF16))
    return pl.pallas_call(
        functools.partial(_ffn_kernel, col_chunk=FFN_COL_CHUNK, final_norm=final_norm),
        out_shape=jax.ShapeDtypeStruct((s, d), F32),
        grid=(s // tm,),
        in_specs=[_row_tile(tm, d), _resident((1, d)), _resident((d, d_ff)), _resident((d, d_ff)),
                  _resident((d_ff, d)), _resident((1, d))],
        out_specs=_row_tile(tm, d),
        scratch_shapes=[pltpu.VMEM((tm, d_ff), BF16)],
        compiler_params=pltpu.CompilerParams(dimension_semantics=("arbitrary",),
                                             vmem_limit_bytes=_vmem_limit(resident)),
        name="ffn",
    )(x, nw, wg, wu, wd, fnw)


def _sc_kernel(x_ref, nw_ref, win_ref, cw_ref, wout_ref, o_ref, cu_ref):
    tm, d = x_ref.shape
    k_w = cw_ref.shape[0]
    head = V7X_SUBLANES

    @pl.when(pl.program_id(0) == 0)
    def _():
        cu_ref[0:head, :] = jnp.zeros((head, d), F32)

    x = x_ref[...]
    h = _rmsnorm(x, nw_ref[...]).astype(BF16)
    cg = _dot(h, win_ref[:, d:2 * d])
    u = _dot(h, win_ref[:, 2 * d:3 * d])
    cu_ref[head:head + tm, :] = cg * u
    v = cu_ref[head:head + tm, :] * cw_ref[k_w - 1:k_w, :]
    for k in range(k_w - 1):
        shift = k_w - 1 - k
        v = v + cu_ref[head - shift:head - shift + tm, :] * cw_ref[k:k + 1, :]
    cu_ref[0:head, :] = cu_ref[tm:tm + head, :]
    bg = _dot(h, win_ref[:, 0:d])
    o_ref[...] = x + _dot((bg * v).astype(BF16), wout_ref[...])


def _shortconv(x, nw, win, cw, wout):
    s, d = x.shape
    tm = SC_ROW_TILE
    assert s % tm == 0
    resident = (4 * _nbytes((tm, d), F32) + _nbytes((d, 3 * d), BF16) + _nbytes((d, d), BF16)
                + _nbytes((tm + V7X_SUBLANES, d), F32))
    return pl.pallas_call(
        _sc_kernel,
        out_shape=jax.ShapeDtypeStruct((s, d), F32),
        grid=(s // tm,),
        in_specs=[_row_tile(tm, d), _resident((1, d)), _resident((d, 3 * d)), _resident(cw.shape),
                  _resident((d, d))],
        out_specs=_row_tile(tm, d),
        scratch_shapes=[pltpu.VMEM((tm + V7X_SUBLANES, d), F32)],
        compiler_params=pltpu.CompilerParams(dimension_semantics=("arbitrary",),
                                             vmem_limit_bytes=_vmem_limit(resident)),
        name="shortconv",
    )(x, nw, win, cw, wout)


def _expand_heads(v, expand_ref):
    hi = v.astype(BF16)
    lo = (v - hi.astype(F32)).astype(BF16)
    e = expand_ref[...]
    return _dot(hi, e) + _dot(lo, e)


def _ssd_kernel(x_ref, nw_ref, wz_ref, wxbc_ref, wdt_ref, wdtT_ref, cw_ref, cb_ref, dtb_ref, dtbT_ref,
                alog_ref, alogT_ref, dskip_ref, gnw_ref, wout_ref, expand_ref,
                o_ref,
                h_ref, conv_ref, xbc_ref, dt_ref, y_ref, state_ref, *, col_chunk):
    tm, d = x_ref.shape
    d_inner = wz_ref.shape[1]
    conv_dim = wxbc_ref.shape[1]
    k_w = cw_ref.shape[0]
    n_state = state_ref.shape[0]
    n_groups = (conv_dim - d_inner) // (2 * n_state)
    group_w = d_inner // n_groups
    chunk = SSD_CHUNK
    pair_w = 2 * SSD_HEAD_DIM
    head = V7X_SUBLANES

    @pl.when(pl.program_id(0) == 0)
    def _():
        conv_ref[0:head, :] = jnp.zeros((head, conv_dim), F32)
        state_ref[...] = jnp.zeros(state_ref.shape, F32)

    x = x_ref[...]
    h = _rmsnorm(x, nw_ref[...]).astype(BF16)
    h_ref[...] = h
    for c in range(conv_dim // col_chunk):
        cols = slice(c * col_chunk, (c + 1) * col_chunk)
        conv_ref[head:head + tm, cols] = _dot(h, wxbc_ref[:, cols])
        acc = conv_ref[head:head + tm, cols] * cw_ref[k_w - 1:k_w, cols] + cb_ref[:, cols]
        for k in range(k_w - 1):
            shift = k_w - 1 - k
            acc = acc + conv_ref[head - shift:head - shift + tm, cols] * cw_ref[k:k + 1, cols]
        xbc_ref[:, cols] = _silu(acc)
    conv_ref[0:head, :] = conv_ref[tm:tm + head, :]
    dt_ref[...] = _softplus(_dot(h, wdt_ref[...]) + dtb_ref[...])

    a_row = -jnp.exp(alog_ref[...])
    a_col = -jnp.exp(alogT_ref[...])
    li = lax.broadcasted_iota(jnp.int32, (chunk, chunk), 0)
    si = lax.broadcasted_iota(jnp.int32, (chunk, chunk), 1)
    causal = li >= si
    tri = causal.astype(F32)
    tri_t = (li <= si).astype(F32)
    lane = lax.broadcasted_iota(jnp.int32, (chunk, pair_w), 1)
    first_head = lane < SSD_HEAD_DIM

    def chunk_body(c, carry):
        r0 = pl.multiple_of(c * chunk, chunk)
        rows = pl.ds(r0, chunk)
        dt = dt_ref[rows, :]
        dt_t = _softplus(lax.dot_general(wdtT_ref[...], h_ref[rows, :], (((1,), (1,)), ((), ())),
                                         preferred_element_type=F32) + dtbT_ref[...])
        a_cs = jnp.dot(tri, dt * a_row, precision=lax.Precision.HIGHEST, preferred_element_type=F32)
        a_cs_t = jnp.dot(dt_t * a_col, tri_t, precision=lax.Precision.HIGHEST, preferred_element_type=F32)
        tot = a_cs[chunk - 1:chunk, :]
        per_head = jnp.concatenate(
            [jnp.exp(tot - a_cs) * dt, jnp.exp(a_cs), jnp.broadcast_to(jnp.exp(tot), (V7X_SUBLANES, tot.shape[1]))],
            axis=0)
        per_col = _expand_heads(per_head, expand_ref)
        w_state = per_col[0:chunk]
        w_off = per_col[chunk:2 * chunk]
        w_tot = per_col[2 * chunk:2 * chunk + 1]
        for g in range(n_groups):
            gcols = slice(g * group_w, (g + 1) * group_w)
            xg = xbc_ref[rows, gcols]
            bg = xbc_ref[rows, d_inner + g * n_state:d_inner + (g + 1) * n_state].astype(BF16)
            cg = xbc_ref[rows, d_inner + (n_groups + g) * n_state:d_inner + (n_groups + g + 1) * n_state].astype(BF16)
            cb = lax.dot_general(cg, bg, (((1,), (1,)), ((), ())), preferred_element_type=F32)
            s_old = state_ref[:, gcols]
            y_off = _dot(cg, s_old.astype(BF16)) * w_off[:, gcols]
            wx = (w_state[:, gcols] * xg).astype(BF16)
            state_ref[:, gcols] = w_tot[:, gcols] * s_old + lax.dot_general(
                bg, wx, (((0,), (0,)), ((), ())), preferred_element_type=F32)
            for j in range(group_w // pair_w):
                h0 = (g * group_w + j * pair_w) // SSD_HEAD_DIM
                pcols = slice(g * group_w + j * pair_w, g * group_w + (j + 1) * pair_w)
                m = []
                for hh in (h0, h0 + 1):
                    seg = a_cs[:, hh:hh + 1] - a_cs_t[hh:hh + 1, :]
                    m.append(cb * jnp.exp(jnp.where(causal, seg, -jnp.inf)) * dt_t[hh:hh + 1, :])
                lhs = jnp.concatenate(m, axis=1).astype(BF16)
                xp = xg[:, j * pair_w:(j + 1) * pair_w]
                rhs = jnp.concatenate([jnp.where(first_head, xp, 0.0), jnp.where(first_head, 0.0, xp)],
                                      axis=0).astype(BF16)
                y_ref[rows, pcols] = (_dot(lhs, rhs) + y_off[:, j * pair_w:(j + 1) * pair_w]
                                      + xp * dskip_ref[:, pcols])
        return carry

    lax.fori_loop(0, tm // chunk, chunk_body, 0)

    gated = y_ref[...] * _silu(_dot(h_ref[...], wz_ref[...]))
    parts = []
    for g in range(n_groups):
        gg = gated[:, g * group_w:(g + 1) * group_w]
        parts.append(gg * lax.rsqrt(jnp.mean(gg * gg, axis=-1, keepdims=True) + RMS_EPS))
    normed = (jnp.concatenate(parts, axis=1) * gnw_ref[...]).astype(BF16)
    o_ref[...] = x + _dot(normed, wout_ref[...])


def _ssd(x, nw, w_in, conv_w, conv_b, dt_bias, a_log, d_skip, norm_w, w_out):
    s, d = x.shape
    d_inner = w_out.shape[0]
    conv_dim = conv_w.shape[1]
    n_heads = dt_bias.shape[0]
    hp = V7X_LANES
    assert n_heads <= hp and n_heads * SSD_HEAD_DIM == d_inner
    tm = SSD_ROW_TILE
    assert s % tm == 0 and tm % SSD_CHUNK == 0 and conv_dim % SSD_COL_CHUNK == 0

    def pad_heads(v):
        return jnp.pad(v, [(0, 0)] * (v.ndim - 1) + [(0, hp - n_heads)])

    wz = w_in[:, :d_inner].astype(BF16)
    wxbc = w_in[:, d_inner:d_inner + conv_dim].astype(BF16)
    wdt = pad_heads(w_in[:, d_inner + conv_dim:]).astype(BF16)
    dtb = pad_heads(dt_bias[None, :])
    alog = pad_heads(a_log[None, :])
    dskip = jnp.repeat(d_skip, SSD_HEAD_DIM)[None, :]
    expand = (jnp.arange(hp)[:, None] == (jnp.arange(d_inner) // SSD_HEAD_DIM)[None, :]).astype(BF16)

    scratch = [
        pltpu.VMEM((tm, d), BF16),
        pltpu.VMEM((tm + V7X_SUBLANES, conv_dim), F32),
        pltpu.VMEM((tm, conv_dim), F32),
        pltpu.VMEM((tm, hp), F32),
        pltpu.VMEM((tm, d_inner), F32),
        pltpu.VMEM((SSD_D_STATE, d_inner), F32),
    ]
    resident = (4 * _nbytes((tm, d), F32) + _nbytes((d, 2 * d_inner + conv_dim + 2 * hp), BF16)
                + _nbytes((hp, d_inner), BF16) + _nbytes((tm, d), BF16)
                + _nbytes((2 * tm + V7X_SUBLANES, conv_dim), F32) + _nbytes((tm, hp), F32)
                + _nbytes((tm + SSD_D_STATE, d_inner), F32))
    return pl.pallas_call(
        functools.partial(_ssd_kernel, col_chunk=SSD_COL_CHUNK),
        out_shape=jax.ShapeDtypeStruct((s, d), F32),
        grid=(s // tm,),
        in_specs=[_row_tile(tm, d), _resident((1, d)), _resident(wz.shape), _resident(wxbc.shape),
                  _resident((d, hp)), _resident((hp, d)), _resident(conv_w.shape), _resident((1, conv_dim)),
                  _resident((1, hp)), _resident((hp, 1)), _resident((1, hp)), _resident((hp, 1)),
                  _resident((1, d_inner)), _resident((1, d_inner)), _resident(w_out.shape),
                  _resident((hp, d_inner))],
        out_specs=_row_tile(tm, d),
        scratch_shapes=scratch,
        compiler_params=pltpu.CompilerParams(dimension_semantics=("arbitrary",),
                                             vmem_limit_bytes=_vmem_limit(resident)),
        name="ssd",
    )(x, nw, wz, wxbc, wdt, wdt.T, conv_w, conv_b[None, :], dtb, dtb.T, alog, alog.T, dskip,
      norm_w[None, :], w_out.astype(BF16), expand)


def kernel(x, norm_w, ffn_w_gate, ffn_w_up, ffn_w_down, ssd_w_in, ssd_conv_w, ssd_conv_b, ssd_dt_bias,
           ssd_a_log, ssd_d, ssd_norm_w, ssd_w_out, sc_w_in, sc_conv_w, sc_w_out, final_norm_w):
    b, s, d = x.shape
    depth = norm_w.shape[0]
    xs = x.reshape(b * s, d)
    assert b == 1, "causal state is carried across row tiles of one sequence"
    fnw = final_norm_w[None, :]
    for i in range(depth):
        j = i // N_MIXERS
        xs = _ffn(xs, norm_w[i, 0][None, :], ffn_w_gate[i, 0].astype(BF16), ffn_w_up[i, 0].astype(BF16),
                  ffn_w_down[i, 0].astype(BF16), fnw, final_norm=False)
        if i % N_MIXERS == 0:
            xs = _ssd(xs, norm_w[i, 1][None, :], ssd_w_in[j], ssd_conv_w[j], ssd_conv_b[j], ssd_dt_bias[j],
                      ssd_a_log[j], ssd_d[j], ssd_norm_w[j], ssd_w_out[j])
        else:
            xs = _shortconv(xs, norm_w[i, 1][None, :], sc_w_in[j].astype(BF16), sc_conv_w[j],
                            sc_w_out[j].astype(BF16))
        xs = _ffn(xs, norm_w[i, 2][None, :], ffn_w_gate[i, 1].astype(BF16), ffn_w_up[i, 1].astype(BF16),
                  ffn_w_down[i, 1].astype(BF16), fnw, final_norm=(i == depth - 1))
    return xs.reshape(b, s, d)
```

```python
import functools
import math

import jax
import jax.numpy as jnp
from jax import lax
from jax.experimental import pallas as pl
from jax.experimental.pallas import tpu as pltpu

F32 = jnp.float32
BF16 = jnp.bfloat16

RMS_EPS = 1e-5
SSD_HEAD_DIM = 64
SSD_N_GROUPS = 4
SSD_D_STATE = 128
SSD_CHUNK = 128
N_MIXERS = 2
LOG2_E = 1.4426950408889634

V7X_LANES = 128
V7X_SUBLANES = 8
V7X_VMEM_BYTES = 64 * 1024 * 1024

FFN_ROW_TILE = 512
FFN_COL_CHUNK = 256
SC_ROW_TILE = 512
SSD_ROW_TILE = 256
SSD_COL_CHUNK = 512
SPLIT_TERMS = 3


def _vmem_limit(resident_bytes):
    return int(min(2 * resident_bytes, V7X_VMEM_BYTES - 8 * 1024 * 1024))


def _nbytes(shape, dtype):
    return math.prod(shape) * jnp.dtype(dtype).itemsize


def _rmsnorm(x, w):
    inv = lax.rsqrt(jnp.mean(x * x, axis=-1, keepdims=True) + RMS_EPS)
    return (x * inv) * w


def _silu(x):
    return x * jax.nn.sigmoid(x)


def _softplus(x):
    return jnp.maximum(x, 0.0) + jnp.log1p(jnp.exp(-jnp.abs(x)))


def _dot(a, b):
    return jnp.dot(a, b, preferred_element_type=F32)


def _bf16_terms(v, n):
    terms = []
    for _ in range(n - 1):
        t = v.astype(BF16)
        terms.append(t)
        v = v - t.astype(F32)
    terms.append(v.astype(BF16))
    return terms


def _causal_conv(ext, w_ref, cols, k_w):
    t, c = ext.shape
    sub = V7X_SUBLANES
    first_sublane = lax.broadcasted_iota(jnp.int32, (t // sub, sub, c), 1) == 0

    def shift_one_row(a):
        r = pltpu.roll(a.reshape(t // sub, sub, c), 1, axis=1)
        return jnp.where(first_sublane, pltpu.roll(r, 1, axis=0), r).reshape(t, c)

    acc = ext * w_ref[0:1, cols]
    for k in range(1, k_w):
        acc = ext * w_ref[k:k + 1, cols] + shift_one_row(acc)
    return acc


def _resident(shape):
    return pl.BlockSpec(shape, lambda i: (0,) * len(shape), pipeline_mode=pl.Buffered(1))


def _row_tile(tm, d):
    return pl.BlockSpec((tm, d), lambda i: (i, 0))


def _ffn_kernel(x_ref, nw_ref, wg_ref, wu_ref, wd_ref, fnw_ref, o_ref, act_ref, *, col_chunk, final_norm):
    x = x_ref[...]
    h = _rmsnorm(x, nw_ref[...]).astype(BF16)
    d_ff = wg_ref.shape[1]
    for c in range(d_ff // col_chunk):
        cols = slice(c * col_chunk, (c + 1) * col_chunk)
        g = _dot(h, wg_ref[:, cols])
        u = _dot(h, wu_ref[:, cols])
        act_ref[:, cols] = (_silu(g) * u).astype(BF16)
    out = x + 0.5 * _dot(act_ref[...], wd_ref[...])
    if final_norm:
        out = _rmsnorm(out, fnw_ref[...])
    o_ref[...] = out


def _ffn(x, nw, wg, wu, wd, fnw, *, final_norm):
    s, d = x.shape
    d_ff = wg.shape[1]
    tm = FFN_ROW_TILE
    assert s % tm == 0 and d_ff % FFN_COL_CHUNK == 0
    resident = (4 * _nbytes((tm, d), F32) + 3 * _nbytes((d, d_ff), BF16) + _nbytes((tm, d_ff), BF16))
    return pl.pallas_call(
        functools.partial(_ffn_kernel, col_chunk=FFN_COL_CHUNK, final_norm=final_norm),
        out_shape=jax.ShapeDtypeStruct((s, d), F32),
        grid=(s // tm,),
        in_specs=[_row_tile(tm, d), _resident((1, d)), _resident((d, d_ff)), _resident((d, d_ff)),
                  _resident((d_ff, d)), _resident((1, d))],
        out_specs=_row_tile(tm, d),
        scratch_shapes=[pltpu.VMEM((tm, d_ff), BF16)],
        compiler_params=pltpu.CompilerParams(dimension_semantics=("arbitrary",),
                                             vmem_limit_bytes=_vmem_limit(resident)),
        name="ffn",
    )(x, nw, wg, wu, wd, fnw)


def _sc_kernel(x_ref, nw_ref, win_ref, cw_ref, wout_ref, o_ref, cu_ref):
    tm, d = x_ref.shape
    k_w = cw_ref.shape[0]
    head = V7X_SUBLANES

    @pl.when(pl.program_id(0) == 0)
    def _():
        cu_ref[0:head, :] = jnp.zeros((head, d), F32)

    x = x_ref[...]
    h = _rmsnorm(x, nw_ref[...]).astype(BF16)
    cg = _dot(h, win_ref[:, d:2 * d])
    u = _dot(h, win_ref[:, 2 * d:3 * d])
    cu_ref[head:head + tm, :] = cg * u
    v = _causal_conv(cu_ref[...], cw_ref, slice(0, d), k_w)[head:head + tm]
    cu_ref[0:head, :] = cu_ref[tm:tm + head, :]
    bg = _dot(h, win_ref[:, 0:d])
    o_ref[...] = x + _dot((bg * v).astype(BF16), wout_ref[...])


def _shortconv(x, nw, win, cw, wout):
    s, d = x.shape
    tm = SC_ROW_TILE
    assert s % tm == 0
    resident = (4 * _nbytes((tm, d), F32) + _nbytes((d, 3 * d), BF16) + _nbytes((d, d), BF16)
                + _nbytes((tm + V7X_SUBLANES, d), F32))
    return pl.pallas_call(
        _sc_kernel,
        out_shape=jax.ShapeDtypeStruct((s, d), F32),
        grid=(s // tm,),
        in_specs=[_row_tile(tm, d), _resident((1, d)), _resident((d, 3 * d)), _resident(cw.shape),
                  _resident((d, d))],
        out_specs=_row_tile(tm, d),
        scratch_shapes=[pltpu.VMEM((tm + V7X_SUBLANES, d), F32)],
        compiler_params=pltpu.CompilerParams(dimension_semantics=("arbitrary",),
                                             vmem_limit_bytes=_vmem_limit(resident)),
        name="shortconv",
    )(x, nw, win, cw, wout)


def _ssd_chunk(rows, xbc_ref, dt_ref, y_ref, state_ref, alog_ref, dskip_ref, tri_ref, expand_ref, *, d_inner):
    chunk = SSD_CHUNK
    n_state = state_ref.shape[0]
    n_groups = (xbc_ref.shape[1] - d_inner) // (2 * n_state)
    group_w = d_inner // n_groups
    pair_w = 2 * SSD_HEAD_DIM

    dt = dt_ref[rows, :]
    ac = dt * (-jnp.exp(alog_ref[...]))
    a_cs = _dot(tri_ref[...], jnp.concatenate(_bf16_terms(ac, SPLIT_TERMS), axis=0))
    a2 = a_cs * LOG2_E
    a2_t = a2.T
    s_side_t = a2_t - jnp.log2(dt.T)
    tot = a_cs[chunk - 1:chunk, :]
    per_head = jnp.concatenate(
        [jnp.exp(tot - a_cs) * dt, jnp.exp(a_cs), jnp.broadcast_to(jnp.exp(tot), (V7X_SUBLANES, tot.shape[1]))],
        axis=0)
    per_col = _dot(jnp.concatenate(_bf16_terms(per_head, 2), axis=1), expand_ref[...])
    w_state = per_col[0:chunk]
    w_off = per_col[chunk:2 * chunk]
    w_tot = per_col[2 * chunk:2 * chunk + 1]

    causal = (lax.broadcasted_iota(jnp.int32, (chunk, chunk), 0)
              >= lax.broadcasted_iota(jnp.int32, (chunk, chunk), 1))
    first_head = lax.broadcasted_iota(jnp.int32, (chunk, pair_w), 1) < SSD_HEAD_DIM
    for g in range(n_groups):
        gcols = slice(g * group_w, (g + 1) * group_w)
        xg = xbc_ref[rows, gcols]
        bg = xbc_ref[rows, d_inner + g * n_state:d_inner + (g + 1) * n_state].astype(BF16)
        cg = xbc_ref[rows, d_inner + (n_groups + g) * n_state:d_inner + (n_groups + g + 1) * n_state].astype(BF16)
        cb = lax.dot_general(cg, bg, (((1,), (1,)), ((), ())), preferred_element_type=F32)
        s_old = state_ref[:, gcols]
        y_off = _dot(cg, s_old.astype(BF16)) * w_off[:, gcols]
        wx = (w_state[:, gcols] * xg).astype(BF16)
        state_ref[:, gcols] = w_tot[:, gcols] * s_old + lax.dot_general(
            bg, wx, (((0,), (0,)), ((), ())), preferred_element_type=F32)
        for j in range(group_w // pair_w):
            h0 = (g * group_w + j * pair_w) // SSD_HEAD_DIM
            pcols = slice(g * group_w + j * pair_w, g * group_w + (j + 1) * pair_w)
            m = []
            for hh in (h0, h0 + 1):
                seg2 = a2[:, hh:hh + 1] - s_side_t[hh:hh + 1, :]
                m.append(cb * jnp.exp2(jnp.where(causal, seg2, -jnp.inf)))
            lhs = jnp.concatenate(m, axis=1).astype(BF16)
            xp = xg[:, j * pair_w:(j + 1) * pair_w]
            rhs = jnp.concatenate([jnp.where(first_head, xp, 0.0), jnp.where(first_head, 0.0, xp)],
                                  axis=0).astype(BF16)
            y_ref[rows, pcols] = (_dot(lhs, rhs) + y_off[:, j * pair_w:(j + 1) * pair_w]
                                  + xp * dskip_ref[:, pcols])


def _ssd_kernel(x_ref, nw_ref, wz_ref, wxbc_ref, wdt_ref, cw_ref, cb_ref, dtb_ref, alog_ref, dskip_ref, gnw_ref,
                wout_ref, tri_ref, expand_ref,
                o_ref,
                h_ref, conv_ref, xbc_ref, dt_ref, y_ref, state_ref, *, col_chunk):
    tm, d = x_ref.shape
    d_inner = wz_ref.shape[1]
    conv_dim = wxbc_ref.shape[1]
    k_w = cw_ref.shape[0]
    n_groups = (conv_dim - d_inner) // (2 * state_ref.shape[0])
    group_w = d_inner // n_groups
    head = V7X_SUBLANES

    @pl.when(pl.program_id(0) == 0)
    def _():
        conv_ref[0:head, :] = jnp.zeros((head, conv_dim), F32)
        state_ref[...] = jnp.zeros(state_ref.shape, F32)

    x = x_ref[...]
    h = _rmsnorm(x, nw_ref[...]).astype(BF16)
    h_ref[...] = h
    for c in range(conv_dim // col_chunk):
        cols = slice(c * col_chunk, (c + 1) * col_chunk)
        conv_ref[head:head + tm, cols] = _dot(h, wxbc_ref[:, cols])
        acc = _causal_conv(conv_ref[:, cols], cw_ref, cols, k_w)[head:head + tm] + cb_ref[:, cols]
        xbc_ref[:, cols] = _silu(acc)
    conv_ref[0:head, :] = conv_ref[tm:tm + head, :]
    dt_ref[...] = _softplus(_dot(h, wdt_ref[...]) + dtb_ref[...])

    for c in range(tm // SSD_CHUNK):
        _ssd_chunk(slice(c * SSD_CHUNK, (c + 1) * SSD_CHUNK), xbc_ref, dt_ref, y_ref, state_ref, alog_ref,
                   dskip_ref, tri_ref, expand_ref, d_inner=d_inner)

    gated = y_ref[...] * _silu(_dot(h_ref[...], wz_ref[...]))
    parts = []
    for g in range(n_groups):
        gg = gated[:, g * group_w:(g + 1) * group_w]
        parts.append(gg * lax.rsqrt(jnp.mean(gg * gg, axis=-1, keepdims=True) + RMS_EPS))
    normed = (jnp.concatenate(parts, axis=1) * gnw_ref[...]).astype(BF16)
    o_ref[...] = x + _dot(normed, wout_ref[...])


def _ssd(x, nw, w_in, conv_w, conv_b, dt_bias, a_log, d_skip, norm_w, w_out):
    s, d = x.shape
    d_inner = w_out.shape[0]
    conv_dim = conv_w.shape[1]
    n_heads = dt_bias.shape[0]
    hp = V7X_LANES
    assert n_heads <= hp and n_heads * SSD_HEAD_DIM == d_inner
    tm = SSD_ROW_TILE
    assert s % tm == 0 and tm % SSD_CHUNK == 0 and conv_dim % SSD_COL_CHUNK == 0

    def pad_heads(v):
        return jnp.pad(v, [(0, 0)] * (v.ndim - 1) + [(0, hp - n_heads)])

    wz = w_in[:, :d_inner].astype(BF16)
    wxbc = w_in[:, d_inner:d_inner + conv_dim].astype(BF16)
    wdt = pad_heads(w_in[:, d_inner + conv_dim:]).astype(BF16)
    dtb = pad_heads(dt_bias[None, :])
    alog = pad_heads(a_log[None, :])
    dskip = jnp.repeat(d_skip, SSD_HEAD_DIM)[None, :]
    tri = jnp.tile(jnp.tril(jnp.ones((SSD_CHUNK, SSD_CHUNK), BF16)), (1, SPLIT_TERMS))
    expand = jnp.tile((jnp.arange(hp)[:, None] == (jnp.arange(d_inner) // SSD_HEAD_DIM)[None, :]).astype(BF16),
                      (2, 1))

    scratch = [
        pltpu.VMEM((tm, d), BF16),
        pltpu.VMEM((tm + V7X_SUBLANES, conv_dim), F32),
        pltpu.VMEM((tm, conv_dim), F32),
        pltpu.VMEM((tm, hp), F32),
        pltpu.VMEM((tm, d_inner), F32),
        pltpu.VMEM((SSD_D_STATE, d_inner), F32),
    ]
    resident = (4 * _nbytes((tm, d), F32) + _nbytes((d, 2 * d_inner + conv_dim + hp), BF16)
                + _nbytes((2 * hp, d_inner), BF16) + _nbytes((tm, d), BF16)
                + _nbytes((2 * tm + V7X_SUBLANES, conv_dim), F32) + _nbytes((tm, hp), F32)
                + _nbytes((tm + SSD_D_STATE, d_inner), F32))
    return pl.pallas_call(
        functools.partial(_ssd_kernel, col_chunk=SSD_COL_CHUNK),
        out_shape=jax.ShapeDtypeStruct((s, d), F32),
        grid=(s // tm,),
        in_specs=[_row_tile(tm, d), _resident((1, d)), _resident(wz.shape), _resident(wxbc.shape),
                  _resident((d, hp)), _resident(conv_w.shape), _resident((1, conv_dim)),
                  _resident((1, hp)), _resident((1, hp)), _resident((1, d_inner)), _resident((1, d_inner)),
                  _resident(w_out.shape), _resident(tri.shape), _resident(expand.shape)],
        out_specs=_row_tile(tm, d),
        scratch_shapes=scratch,
        compiler_params=pltpu.CompilerParams(dimension_semantics=("arbitrary",),
                                             vmem_limit_bytes=_vmem_limit(resident)),
        name="ssd",
    )(x, nw, wz, wxbc, wdt, conv_w, conv_b[None, :], dtb, alog, dskip, norm_w[None, :], w_out.astype(BF16),
      tri, expand)


def kernel(x, norm_w, ffn_w_gate, ffn_w_up, ffn_w_down, ssd_w_in, ssd_conv_w, ssd_conv_b, ssd_dt_bias,
           ssd_a_log, ssd_d, ssd_norm_w, ssd_w_out, sc_w_in, sc_conv_w, sc_w_out, final_norm_w):
    b, s, d = x.shape
    depth = norm_w.shape[0]
    xs = x.reshape(b * s, d)
    assert b == 1, "causal state is carried across row tiles of one sequence"
    fnw = final_norm_w[None, :]
    for i in range(depth):
        j = i // N_MIXERS
        xs = _ffn(xs, norm_w[i, 0][None, :], ffn_w_gate[i, 0].astype(BF16), ffn_w_up[i, 0].astype(BF16),
                  ffn_w_down[i, 0].astype(BF16), fnw, final_norm=False)
        if i % N_MIXERS == 0:
            xs = _ssd(xs, norm_w[i, 1][None, :], ssd_w_in[j], ssd_conv_w[j], ssd_conv_b[j], ssd_dt_bias[j],
                      ssd_a_log[j], ssd_d[j], ssd_norm_w[j], ssd_w_out[j])
        else:
            xs = _shortconv(xs, norm_w[i, 1][None, :], sc_w_in[j].astype(BF16), sc_conv_w[j],
                            sc_w_out[j].astype(BF16))
        xs = _ffn(xs, norm_w[i, 2][None, :], ffn_w_gate[i, 1].astype(BF16), ffn_w_up[i, 1].astype(BF16),
                  ffn_w_down[i, 1].astype(BF16), fnw, final_norm=(i == depth - 1))
    return xs.reshape(b, s, d)
```

```python
import functools
import math

import jax
import jax.numpy as jnp
from jax import lax
from jax.experimental import pallas as pl
from jax.experimental.pallas import tpu as pltpu

F32 = jnp.float32
BF16 = jnp.bfloat16

RMS_EPS = 1e-5
SSD_HEAD_DIM = 64
SSD_N_GROUPS = 4
SSD_D_STATE = 128
SSD_CHUNK = 128
N_MIXERS = 2
LOG2_E = 1.4426950408889634

V7X_LANES = 128
V7X_SUBLANES = 8
V7X_VMEM_BYTES = 64 * 1024 * 1024

FFN_ROW_TILE = 512
FFN_COL_CHUNK = 256
SC_ROW_TILE = 512
SSD_ROW_TILE = 256
SSD_COL_CHUNK = 512
SPLIT_TERMS = 3


def _vmem_limit(resident_bytes):
    return int(min(2 * resident_bytes, V7X_VMEM_BYTES - 8 * 1024 * 1024))


def _nbytes(shape, dtype):
    return math.prod(shape) * jnp.dtype(dtype).itemsize


def _rmsnorm(x, w):
    inv = lax.rsqrt(jnp.mean(x * x, axis=-1, keepdims=True) + RMS_EPS)
    return (x * inv) * w


def _silu(x):
    return x * jax.nn.sigmoid(x)


def _softplus(x):
    return jnp.maximum(x, 0.0) + jnp.log1p(jnp.exp(-jnp.abs(x)))


def _dot(a, b):
    return jnp.dot(a, b, preferred_element_type=F32)


def _bf16_terms(v, n):
    terms = []
    for _ in range(n - 1):
        t = v.astype(BF16)
        terms.append(t)
        v = v - t.astype(F32)
    terms.append(v.astype(BF16))
    return terms


def _causal_conv(ext, w_ref, cols, k_w):
    t, c = ext.shape
    sub = V7X_SUBLANES
    first_sublane = lax.broadcasted_iota(jnp.int32, (t // sub, sub, c), 1) == 0

    def shift_one_row(a):
        r = pltpu.roll(a.reshape(t // sub, sub, c), 1, axis=1)
        return jnp.where(first_sublane, pltpu.roll(r, 1, axis=0), r).reshape(t, c)

    acc = ext * w_ref[0:1, cols]
    for k in range(1, k_w):
        acc = ext * w_ref[k:k + 1, cols] + shift_one_row(acc)
    return acc


def _resident(shape):
    return pl.BlockSpec(shape, lambda i: (0,) * len(shape), pipeline_mode=pl.Buffered(1))


def _resident_layer(stacked, index):
    lead = len(index)
    shape = (None,) * lead + tuple(stacked.shape[lead:])
    return pl.BlockSpec(shape, lambda i: tuple(index) + (0,) * (stacked.ndim - lead),
                        pipeline_mode=pl.Buffered(1))


def _row_tile(tm, d):
    return pl.BlockSpec((tm, d), lambda i: (i, 0))


def _ffn_kernel(x_ref, nw_ref, wg_ref, wu_ref, wd_ref, fnw_ref, o_ref, act_ref, *, col_chunk, final_norm):
    x = x_ref[...]
    h = _rmsnorm(x, nw_ref[...]).astype(BF16)
    d_ff = wg_ref.shape[1]
    for c in range(d_ff // col_chunk):
        cols = slice(c * col_chunk, (c + 1) * col_chunk)
        g = _dot(h, wg_ref[:, cols].astype(BF16))
        u = _dot(h, wu_ref[:, cols].astype(BF16))
        act_ref[:, cols] = (_silu(g) * u).astype(BF16)
    out = x + 0.5 * _dot(act_ref[...], wd_ref[...].astype(BF16))
    if final_norm:
        out = _rmsnorm(out, fnw_ref[...])
    o_ref[...] = out


def _ffn(x, nw, wg, wu, wd, fnw, layer, *, final_norm):
    s, d = x.shape
    d_ff = wg.shape[-1]
    tm = FFN_ROW_TILE
    assert s % tm == 0 and d_ff % FFN_COL_CHUNK == 0
    resident = (4 * _nbytes((tm, d), F32) + 3 * _nbytes((d, d_ff), F32) + _nbytes((tm, d_ff), BF16))
    return pl.pallas_call(
        functools.partial(_ffn_kernel, col_chunk=FFN_COL_CHUNK, final_norm=final_norm),
        out_shape=jax.ShapeDtypeStruct((s, d), F32),
        grid=(s // tm,),
        in_specs=[_row_tile(tm, d), _resident((1, d)), _resident_layer(wg, layer), _resident_layer(wu, layer),
                  _resident_layer(wd, layer), _resident((1, d))],
        out_specs=_row_tile(tm, d),
        scratch_shapes=[pltpu.VMEM((tm, d_ff), BF16)],
        compiler_params=pltpu.CompilerParams(dimension_semantics=("arbitrary",),
                                             vmem_limit_bytes=_vmem_limit(resident)),
        name="ffn",
    )(x, nw, wg, wu, wd, fnw)


def _sc_kernel(x_ref, nw_ref, win_ref, cw_ref, wout_ref, o_ref, cu_ref):
    tm, d = x_ref.shape
    k_w = cw_ref.shape[0]
    head = V7X_SUBLANES

    @pl.when(pl.program_id(0) == 0)
    def _():
        cu_ref[0:head, :] = jnp.zeros((head, d), F32)

    x = x_ref[...]
    h = _rmsnorm(x, nw_ref[...]).astype(BF16)
    cg = _dot(h, win_ref[:, d:2 * d].astype(BF16))
    u = _dot(h, win_ref[:, 2 * d:3 * d].astype(BF16))
    cu_ref[head:head + tm, :] = cg * u
    v = _causal_conv(cu_ref[...], cw_ref, slice(0, d), k_w)[head:head + tm]
    cu_ref[0:head, :] = cu_ref[tm:tm + head, :]
    bg = _dot(h, win_ref[:, 0:d].astype(BF16))
    o_ref[...] = x + _dot((bg * v).astype(BF16), wout_ref[...].astype(BF16))


def _shortconv(x, nw, win, cw, wout, layer):
    s, d = x.shape
    tm = SC_ROW_TILE
    assert s % tm == 0
    resident = (4 * _nbytes((tm, d), F32) + _nbytes((d, 3 * d), F32) + _nbytes((d, d), F32)
                + _nbytes((tm + V7X_SUBLANES, d), F32))
    return pl.pallas_call(
        _sc_kernel,
        out_shape=jax.ShapeDtypeStruct((s, d), F32),
        grid=(s // tm,),
        in_specs=[_row_tile(tm, d), _resident((1, d)), _resident_layer(win, layer), _resident_layer(cw, layer),
                  _resident_layer(wout, layer)],
        out_specs=_row_tile(tm, d),
        scratch_shapes=[pltpu.VMEM((tm + V7X_SUBLANES, d), F32)],
        compiler_params=pltpu.CompilerParams(dimension_semantics=("arbitrary",),
                                             vmem_limit_bytes=_vmem_limit(resident)),
        name="shortconv",
    )(x, nw, win, cw, wout)


def _ssd_chunk(rows, xbc_ref, dt_ref, y_ref, state_ref, alog_ref, dskip_ref, tri_ref, expand_ref, *, d_inner):
    chunk = SSD_CHUNK
    n_state = state_ref.shape[0]
    n_groups = (xbc_ref.shape[1] - d_inner) // (2 * n_state)
    group_w = d_inner // n_groups
    pair_w = 2 * SSD_HEAD_DIM

    dt = dt_ref[rows, :]
    ac = dt * (-jnp.exp(alog_ref[...]))
    a_cs = _dot(tri_ref[...], jnp.concatenate(_bf16_terms(ac, SPLIT_TERMS), axis=0))
    a2 = a_cs * LOG2_E
    a2_t = a2.T
    s_side_t = a2_t - jnp.log2(dt.T)
    tot = a_cs[chunk - 1:chunk, :]
    per_head = jnp.concatenate(
        [jnp.exp(tot - a_cs) * dt, jnp.exp(a_cs), jnp.broadcast_to(jnp.exp(tot), (V7X_SUBLANES, tot.shape[1]))],
        axis=0)
    per_col = _dot(jnp.concatenate(_bf16_terms(per_head, 2), axis=1), expand_ref[...])
    w_state = per_col[0:chunk]
    w_off = per_col[chunk:2 * chunk]
    w_tot = per_col[2 * chunk:2 * chunk + 1]

    causal = (lax.broadcasted_iota(jnp.int32, (chunk, chunk), 0)
              >= lax.broadcasted_iota(jnp.int32, (chunk, chunk), 1))
    first_head = lax.broadcasted_iota(jnp.int32, (chunk, pair_w), 1) < SSD_HEAD_DIM
    for g in range(n_groups):
        gcols = slice(g * group_w, (g + 1) * group_w)
        xg = xbc_ref[rows, gcols]
        bg = xbc_ref[rows, d_inner + g * n_state:d_inner + (g + 1) * n_state].astype(BF16)
        cg = xbc_ref[rows, d_inner + (n_groups + g) * n_state:d_inner + (n_groups + g + 1) * n_state].astype(BF16)
        cb = lax.dot_general(cg, bg, (((1,), (1,)), ((), ())), preferred_element_type=F32)
        s_old = state_ref[:, gcols]
        y_off = _dot(cg, s_old.astype(BF16)) * w_off[:, gcols]
        wx = (w_state[:, gcols] * xg).astype(BF16)
        state_ref[:, gcols] = w_tot[:, gcols] * s_old + lax.dot_general(
            bg, wx, (((0,), (0,)), ((), ())), preferred_element_type=F32)
        for j in range(group_w // pair_w):
            h0 = (g * group_w + j * pair_w) // SSD_HEAD_DIM
            pcols = slice(g * group_w + j * pair_w, g * group_w + (j + 1) * pair_w)
            m = []
            for hh in (h0, h0 + 1):
                seg2 = a2[:, hh:hh + 1] - s_side_t[hh:hh + 1, :]
                m.append(cb * jnp.exp2(jnp.where(causal, seg2, -jnp.inf)))
            lhs = jnp.concatenate(m, axis=1).astype(BF16)
            xp = xg[:, j * pair_w:(j + 1) * pair_w]
            rhs = jnp.concatenate([jnp.where(first_head, xp, 0.0), jnp.where(first_head, 0.0, xp)],
                                  axis=0).astype(BF16)
            y_ref[rows, pcols] = (_dot(lhs, rhs) + y_off[:, j * pair_w:(j + 1) * pair_w]
                                  + xp * dskip_ref[:, pcols])


def _ssd_kernel(x_ref, nw_ref, win_ref, wdt_ref, cw_ref, cb_ref, dtb_ref, alog_ref, dskip_ref, gnw_ref,
                wout_ref, tri_ref, expand_ref,
                o_ref,
                h_ref, conv_ref, xbc_ref, dt_ref, y_ref, state_ref, *, col_chunk):
    tm, d = x_ref.shape
    d_inner = wout_ref.shape[0]
    conv_dim = cw_ref.shape[1]
    k_w = cw_ref.shape[0]
    n_groups = (conv_dim - d_inner) // (2 * state_ref.shape[0])
    group_w = d_inner // n_groups
    head = V7X_SUBLANES

    @pl.when(pl.program_id(0) == 0)
    def _():
        conv_ref[0:head, :] = jnp.zeros((head, conv_dim), F32)
        state_ref[...] = jnp.zeros(state_ref.shape, F32)

    x = x_ref[...]
    h = _rmsnorm(x, nw_ref[...]).astype(BF16)
    h_ref[...] = h
    for c in range(conv_dim // col_chunk):
        cols = slice(c * col_chunk, (c + 1) * col_chunk)
        conv_ref[head:head + tm, cols] = _dot(
            h, win_ref[:, d_inner + c * col_chunk:d_inner + (c + 1) * col_chunk].astype(BF16))
        acc = _causal_conv(conv_ref[:, cols], cw_ref, cols, k_w)[head:head + tm] + cb_ref[:, cols]
        xbc_ref[:, cols] = _silu(acc)
    conv_ref[0:head, :] = conv_ref[tm:tm + head, :]
    dt_ref[...] = _softplus(_dot(h, wdt_ref[...]) + dtb_ref[...])

    for c in range(tm // SSD_CHUNK):
        _ssd_chunk(slice(c * SSD_CHUNK, (c + 1) * SSD_CHUNK), xbc_ref, dt_ref, y_ref, state_ref, alog_ref,
                   dskip_ref, tri_ref, expand_ref, d_inner=d_inner)

    gated = y_ref[...] * _silu(_dot(h_ref[...], win_ref[:, 0:d_inner].astype(BF16)))
    parts = []
    for g in range(n_groups):
        gg = gated[:, g * group_w:(g + 1) * group_w]
        parts.append(gg * lax.rsqrt(jnp.mean(gg * gg, axis=-1, keepdims=True) + RMS_EPS))
    normed = (jnp.concatenate(parts, axis=1) * gnw_ref[...]).astype(BF16)
    o_ref[...] = x + _dot(normed, wout_ref[...].astype(BF16))


def _ssd(x, nw, w_in, conv_w, conv_b, dt_bias, a_log, d_skip, norm_w, w_out, layer):
    s, d = x.shape
    d_inner = w_out.shape[-2]
    conv_dim = conv_w.shape[-1]
    n_heads = dt_bias.shape[0]
    hp = V7X_LANES
    assert n_heads <= hp and n_heads * SSD_HEAD_DIM == d_inner
    tm = SSD_ROW_TILE
    assert s % tm == 0 and tm % SSD_CHUNK == 0 and conv_dim % SSD_COL_CHUNK == 0

    def pad_heads(v):
        return jnp.pad(v, [(0, 0)] * (v.ndim - 1) + [(0, hp - n_heads)])

    wdt = pad_heads(w_in[layer][:, d_inner + conv_dim:]).astype(BF16)
    dtb = pad_heads(dt_bias[None, :])
    alog = pad_heads(a_log[None, :])
    dskip = jnp.repeat(d_skip, SSD_HEAD_DIM)[None, :]
    tri = jnp.tile(jnp.tril(jnp.ones((SSD_CHUNK, SSD_CHUNK), BF16)), (1, SPLIT_TERMS))
    expand = jnp.tile((jnp.arange(hp)[:, None] == (jnp.arange(d_inner) // SSD_HEAD_DIM)[None, :]).astype(BF16),
                      (2, 1))

    scratch = [
        pltpu.VMEM((tm, d), BF16),
        pltpu.VMEM((tm + V7X_SUBLANES, conv_dim), F32),
        pltpu.VMEM((tm, conv_dim), F32),
        pltpu.VMEM((tm, hp), F32),
        pltpu.VMEM((tm, d_inner), F32),
        pltpu.VMEM((SSD_D_STATE, d_inner), F32),
    ]
    resident = (4 * _nbytes((tm, d), F32) + _nbytes(w_in.shape[1:], F32) + _nbytes(w_out.shape[1:], F32)
                + _nbytes((d, hp), BF16) + _nbytes((2 * hp, d_inner), BF16) + _nbytes((tm, d), BF16)
                + _nbytes((2 * tm + V7X_SUBLANES, conv_dim), F32) + _nbytes((tm, hp), F32)
                + _nbytes((tm + SSD_D_STATE, d_inner), F32))
    return pl.pallas_call(
        functools.partial(_ssd_kernel, col_chunk=SSD_COL_CHUNK),
        out_shape=jax.ShapeDtypeStruct((s, d), F32),
        grid=(s // tm,),
        in_specs=[_row_tile(tm, d), _resident((1, d)), _resident_layer(w_in, layer),
                  _resident((d, hp)), _resident_layer(conv_w, layer), _resident((1, conv_dim)),
                  _resident((1, hp)), _resident((1, hp)), _resident((1, d_inner)), _resident((1, d_inner)),
                  _resident_layer(w_out, layer), _resident(tri.shape), _resident(expand.shape)],
        out_specs=_row_tile(tm, d),
        scratch_shapes=scratch,
        compiler_params=pltpu.CompilerParams(dimension_semantics=("arbitrary",),
                                             vmem_limit_bytes=_vmem_limit(resident)),
        name="ssd",
    )(x, nw, w_in, wdt, conv_w, conv_b[None, :], dtb, alog, dskip, norm_w[None, :], w_out, tri, expand)


def kernel(x, norm_w, ffn_w_gate, ffn_w_up, ffn_w_down, ssd_w_in, ssd_conv_w, ssd_conv_b, ssd_dt_bias,
           ssd_a_log, ssd_d, ssd_norm_w, ssd_w_out, sc_w_in, sc_conv_w, sc_w_out, final_norm_w):
    b, s, d = x.shape
    depth = norm_w.shape[0]
    xs = x.reshape(b * s, d)
    assert b == 1, "causal state is carried across row tiles of one sequence"
    fnw = final_norm_w[None, :]
    for i in range(depth):
        j = i // N_MIXERS
        xs = _ffn(xs, norm_w[i, 0][None, :], ffn_w_gate, ffn_w_up, ffn_w_down, fnw, (i, 0), final_norm=False)
        if i % N_MIXERS == 0:
            xs = _ssd(xs, norm_w[i, 1][None, :], ssd_w_in, ssd_conv_w, ssd_conv_b[j], ssd_dt_bias[j],
                      ssd_a_log[j], ssd_d[j], ssd_norm_w[j], ssd_w_out, (j,))
        else:
            xs = _shortconv(xs, norm_w[i, 1][None, :], sc_w_in, sc_conv_w, sc_w_out, (j,))
        xs = _ffn(xs, norm_w[i, 2][None, :], ffn_w_gate, ffn_w_up, ffn_w_down, fnw, (i, 1),
                  final_norm=(i == depth - 1))
    return xs.reshape(b, s, d)
```

```python
import functools
import math

import jax
import jax.numpy as jnp
from jax import lax
from jax.experimental import pallas as pl
from jax.experimental.pallas import tpu as pltpu

F32 = jnp.float32
BF16 = jnp.bfloat16

RMS_EPS = 1e-5
SSD_HEAD_DIM = 64
SSD_N_GROUPS = 4
SSD_D_STATE = 128
SSD_CHUNK = 128
N_MIXERS = 2
LOG2_E = 1.4426950408889634

V7X_LANES = 128
V7X_SUBLANES = 8
V7X_VMEM_BYTES = 64 * 1024 * 1024

FFN_ROW_TILE = 512
FFN_COL_CHUNK = 256
SC_ROW_TILE = 512
SSD_ROW_TILE = 256
SSD_COL_CHUNK = 512
SPLIT_TERMS = 3


def _vmem_limit(resident_bytes):
    return int(min(2 * resident_bytes, V7X_VMEM_BYTES - 8 * 1024 * 1024))


def _nbytes(shape, dtype):
    return math.prod(shape) * jnp.dtype(dtype).itemsize


def _rmsnorm(x, w):
    inv = lax.rsqrt(jnp.mean(x * x, axis=-1, keepdims=True) + RMS_EPS)
    return (x * inv) * w


def _silu(x):
    return x * jax.nn.sigmoid(x)


def _softplus(x):
    return jnp.maximum(x, 0.0) + jnp.log1p(jnp.exp(-jnp.abs(x)))


def _dot(a, b):
    return jnp.dot(a, b, preferred_element_type=F32)


def _bf16_terms(v, n):
    terms = []
    for _ in range(n - 1):
        t = v.astype(BF16)
        terms.append(t)
        v = v - t.astype(F32)
    terms.append(v.astype(BF16))
    return terms


def _causal_conv(ext, w_ref, cols, k_w):
    t, c = ext.shape
    sub = V7X_SUBLANES
    last_sublane = lax.broadcasted_iota(jnp.int32, (t // sub, sub, c), 1) == sub - 1

    def shift_one_row(a):
        a = a.reshape(t // sub, sub, c)
        return pltpu.roll(jnp.where(last_sublane, pltpu.roll(a, 1, axis=0), a), 1, axis=1).reshape(t, c)

    acc = ext * w_ref[0:1, cols]
    for k in range(1, k_w):
        acc = ext * w_ref[k:k + 1, cols] + shift_one_row(acc)
    return acc


def _resident(shape):
    return pl.BlockSpec(shape, lambda i: (0,) * len(shape), pipeline_mode=pl.Buffered(1))


def _resident_layer(stacked, index):
    lead = len(index)
    shape = (None,) * lead + tuple(stacked.shape[lead:])
    return pl.BlockSpec(shape, lambda i: tuple(index) + (0,) * (stacked.ndim - lead),
                        pipeline_mode=pl.Buffered(1))


def _row_tile(tm, d):
    return pl.BlockSpec((tm, d), lambda i: (i, 0))


def _ffn_kernel(x_ref, nw_ref, wg_ref, wu_ref, wd_ref, fnw_ref, o_ref, act_ref, *, col_chunk, final_norm):
    x = x_ref[...]
    h = _rmsnorm(x, nw_ref[...]).astype(BF16)
    d_ff = wg_ref.shape[1]
    for c in range(d_ff // col_chunk):
        cols = slice(c * col_chunk, (c + 1) * col_chunk)
        g = _dot(h, wg_ref[:, cols].astype(BF16))
        u = _dot(h, wu_ref[:, cols].astype(BF16))
        act_ref[:, cols] = (_silu(g) * u).astype(BF16)
    out = x + 0.5 * _dot(act_ref[...], wd_ref[...].astype(BF16))
    if final_norm:
        out = _rmsnorm(out, fnw_ref[...])
    o_ref[...] = out


def _ffn(x, nw, wg, wu, wd, fnw, layer, *, final_norm):
    s, d = x.shape
    d_ff = wg.shape[-1]
    tm = FFN_ROW_TILE
    assert s % tm == 0 and d_ff % FFN_COL_CHUNK == 0
    resident = (4 * _nbytes((tm, d), F32) + 3 * _nbytes((d, d_ff), F32) + _nbytes((tm, d_ff), BF16))
    return pl.pallas_call(
        functools.partial(_ffn_kernel, col_chunk=FFN_COL_CHUNK, final_norm=final_norm),
        out_shape=jax.ShapeDtypeStruct((s, d), F32),
        grid=(s // tm,),
        in_specs=[_row_tile(tm, d), _resident((1, d)), _resident_layer(wg, layer), _resident_layer(wu, layer),
                  _resident_layer(wd, layer), _resident((1, d))],
        out_specs=_row_tile(tm, d),
        scratch_shapes=[pltpu.VMEM((tm, d_ff), BF16)],
        compiler_params=pltpu.CompilerParams(dimension_semantics=("arbitrary",),
                                             vmem_limit_bytes=_vmem_limit(resident)),
        name="ffn",
    )(x, nw, wg, wu, wd, fnw)


def _sc_kernel(x_ref, nw_ref, win_ref, cw_ref, wout_ref, o_ref, cu_ref):
    tm, d = x_ref.shape
    k_w = cw_ref.shape[0]
    head = V7X_SUBLANES

    @pl.when(pl.program_id(0) == 0)
    def _():
        cu_ref[0:head, :] = jnp.zeros((head, d), F32)

    x = x_ref[...]
    h = _rmsnorm(x, nw_ref[...]).astype(BF16)
    cg = _dot(h, win_ref[:, d:2 * d].astype(BF16))
    u = _dot(h, win_ref[:, 2 * d:3 * d].astype(BF16))
    cu_ref[head:head + tm, :] = cg * u
    v = _causal_conv(cu_ref[...], cw_ref, slice(0, d), k_w)[head:head + tm]
    cu_ref[0:head, :] = cu_ref[tm:tm + head, :]
    bg = _dot(h, win_ref[:, 0:d].astype(BF16))
    o_ref[...] = x + _dot((bg * v).astype(BF16), wout_ref[...].astype(BF16))


def _shortconv(x, nw, win, cw, wout, layer):
    s, d = x.shape
    tm = SC_ROW_TILE
    assert s % tm == 0
    resident = (4 * _nbytes((tm, d), F32) + _nbytes((d, 3 * d), F32) + _nbytes((d, d), F32)
                + _nbytes((tm + V7X_SUBLANES, d), F32))
    return pl.pallas_call(
        _sc_kernel,
        out_shape=jax.ShapeDtypeStruct((s, d), F32),
        grid=(s // tm,),
        in_specs=[_row_tile(tm, d), _resident((1, d)), _resident_layer(win, layer), _resident_layer(cw, layer),
                  _resident_layer(wout, layer)],
        out_specs=_row_tile(tm, d),
        scratch_shapes=[pltpu.VMEM((tm + V7X_SUBLANES, d), F32)],
        compiler_params=pltpu.CompilerParams(dimension_semantics=("arbitrary",),
                                             vmem_limit_bytes=_vmem_limit(resident)),
        name="shortconv",
    )(x, nw, win, cw, wout)


def _ssd_chunk_decays(rows, dt_ref, alog_ref, tri_ref, expand_ref):
    chunk = SSD_CHUNK
    dt = dt_ref[rows, :]
    ac = dt * (-jnp.exp(alog_ref[...]))
    a_cs = _dot(tri_ref[...], jnp.concatenate(_bf16_terms(ac, SPLIT_TERMS), axis=0))
    a2 = a_cs * LOG2_E
    s_side_t = a2.T - jnp.log2(dt.T)
    tot = a_cs[chunk - 1:chunk, :]
    per_head = jnp.concatenate(
        [jnp.exp(tot - a_cs) * dt, jnp.exp(a_cs), jnp.broadcast_to(jnp.exp(tot), (V7X_SUBLANES, tot.shape[1]))],
        axis=0)
    per_col = _dot(jnp.concatenate(_bf16_terms(per_head, 2), axis=1), expand_ref[...])
    return a2, s_side_t, per_col[0:chunk], per_col[chunk:2 * chunk], per_col[2 * chunk:2 * chunk + 1]


def _ssd_chunk_group(g, rows, decays, xbc_ref, y_ref, state_ref, dskip_ref, *, d_inner):
    a2, s_side_t, w_state, w_off, w_tot = decays
    chunk = SSD_CHUNK
    n_state = state_ref.shape[0]
    n_groups = (xbc_ref.shape[1] - d_inner) // (2 * n_state)
    group_w = d_inner // n_groups
    pair_w = 2 * SSD_HEAD_DIM
    causal = (lax.broadcasted_iota(jnp.int32, (chunk, chunk), 0)
              >= lax.broadcasted_iota(jnp.int32, (chunk, chunk), 1))
    first_head = lax.broadcasted_iota(jnp.int32, (chunk, pair_w), 1) < SSD_HEAD_DIM

    gcols = slice(g * group_w, (g + 1) * group_w)
    xg = xbc_ref[rows, gcols]
    bg = xbc_ref[rows, d_inner + g * n_state:d_inner + (g + 1) * n_state].astype(BF16)
    cg = xbc_ref[rows, d_inner + (n_groups + g) * n_state:d_inner + (n_groups + g + 1) * n_state].astype(BF16)
    cb = lax.dot_general(cg, bg, (((1,), (1,)), ((), ())), preferred_element_type=F32)
    s_old = state_ref[:, gcols]
    y_off = _dot(cg, s_old.astype(BF16)) * w_off[:, gcols]
    wx = (w_state[:, gcols] * xg).astype(BF16)
    state_ref[:, gcols] = w_tot[:, gcols] * s_old + lax.dot_general(
        bg, wx, (((0,), (0,)), ((), ())), preferred_element_type=F32)
    for j in range(group_w // pair_w):
        h0 = (g * group_w + j * pair_w) // SSD_HEAD_DIM
        pcols = slice(g * group_w + j * pair_w, g * group_w + (j + 1) * pair_w)
        m = []
        for hh in (h0, h0 + 1):
            seg2 = a2[:, hh:hh + 1] - s_side_t[hh:hh + 1, :]
            m.append(cb * jnp.exp2(jnp.where(causal, seg2, -jnp.inf)))
        lhs = jnp.concatenate(m, axis=1).astype(BF16)
        xp = xg[:, j * pair_w:(j + 1) * pair_w]
        rhs = jnp.concatenate([jnp.where(first_head, xp, 0.0), jnp.where(first_head, 0.0, xp)],
                              axis=0).astype(BF16)
        y_ref[rows, pcols] = (_dot(lhs, rhs) + y_off[:, j * pair_w:(j + 1) * pair_w]
                              + xp * dskip_ref[:, pcols])


def _ssd_kernel(x_ref, nw_ref, win_ref, wdt_ref, cw_ref, cb_ref, dtb_ref, alog_ref, dskip_ref, gnw_ref,
                wout_ref, tri_ref, expand_ref,
                o_ref,
                h_ref, conv_ref, xbc_ref, dt_ref, y_ref, state_ref, *, col_chunk):
    tm, d = x_ref.shape
    d_inner = wout_ref.shape[0]
    conv_dim = cw_ref.shape[1]
    k_w = cw_ref.shape[0]
    n_groups = (conv_dim - d_inner) // (2 * state_ref.shape[0])
    group_w = d_inner // n_groups
    head = V7X_SUBLANES

    @pl.when(pl.program_id(0) == 0)
    def _():
        conv_ref[0:head, :] = jnp.zeros((head, conv_dim), F32)
        state_ref[...] = jnp.zeros(state_ref.shape, F32)

    x = x_ref[...]
    h = _rmsnorm(x, nw_ref[...]).astype(BF16)
    h_ref[...] = h
    for c in range(conv_dim // col_chunk):
        cols = slice(c * col_chunk, (c + 1) * col_chunk)
        conv_ref[head:head + tm, cols] = _dot(
            h, win_ref[:, d_inner + c * col_chunk:d_inner + (c + 1) * col_chunk])
        acc = _causal_conv(conv_ref[:, cols], cw_ref, cols, k_w)[head:head + tm] + cb_ref[:, cols]
        xbc_ref[:, cols] = _silu(acc)
    conv_ref[0:head, :] = conv_ref[tm:tm + head, :]
    dt_ref[...] = _softplus(_dot(h, wdt_ref[...]) + dtb_ref[...])

    chunks = [slice(c * SSD_CHUNK, (c + 1) * SSD_CHUNK) for c in range(tm // SSD_CHUNK)]
    decays = [_ssd_chunk_decays(rows, dt_ref, alog_ref, tri_ref, expand_ref) for rows in chunks]
    out = x
    for g in range(n_groups):
        gcols = slice(g * group_w, (g + 1) * group_w)
        zs = _silu(_dot(h_ref[...], win_ref[:, gcols]))
        for rows, dec in zip(chunks, decays):
            _ssd_chunk_group(g, rows, dec, xbc_ref, y_ref, state_ref, dskip_ref, d_inner=d_inner)
        gated = y_ref[:, gcols] * zs
        inv = lax.rsqrt(jnp.mean(gated * gated, axis=-1, keepdims=True) + RMS_EPS)
        normed = ((gated * inv) * gnw_ref[:, gcols]).astype(BF16)
        out = out + _dot(normed, wout_ref[gcols, :].astype(BF16))
    o_ref[...] = out


def _ssd(x, nw, w_in, conv_w, conv_b, dt_bias, a_log, d_skip, norm_w, w_out, layer):
    s, d = x.shape
    d_inner = w_out.shape[-2]
    conv_dim = conv_w.shape[-1]
    n_heads = dt_bias.shape[0]
    hp = V7X_LANES
    assert n_heads <= hp and n_heads * SSD_HEAD_DIM == d_inner
    tm = SSD_ROW_TILE
    assert s % tm == 0 and tm % SSD_CHUNK == 0 and conv_dim % SSD_COL_CHUNK == 0

    def pad_heads(v):
        return jnp.pad(v, [(0, 0)] * (v.ndim - 1) + [(0, hp - n_heads)])

    wdt = pad_heads(w_in[layer][:, d_inner + conv_dim:])
    dtb = pad_heads(dt_bias[None, :])
    alog = pad_heads(a_log[None, :])
    dskip = jnp.repeat(d_skip, SSD_HEAD_DIM)[None, :]
    tri = jnp.tile(jnp.tril(jnp.ones((SSD_CHUNK, SSD_CHUNK), BF16)), (1, SPLIT_TERMS))
    expand = jnp.tile((jnp.arange(hp)[:, None] == (jnp.arange(d_inner) // SSD_HEAD_DIM)[None, :]).astype(BF16),
                      (2, 1))

    scratch = [
        pltpu.VMEM((tm, d), BF16),
        pltpu.VMEM((tm + V7X_SUBLANES, conv_dim), F32),
        pltpu.VMEM((tm, conv_dim), F32),
        pltpu.VMEM((tm, hp), F32),
        pltpu.VMEM((tm, d_inner), F32),
        pltpu.VMEM((SSD_D_STATE, d_inner), F32),
    ]
    resident = (4 * _nbytes((tm, d), F32) + _nbytes(w_in.shape[1:], BF16) + _nbytes(w_out.shape[1:], F32)
                + _nbytes((d, hp), BF16) + _nbytes((2 * hp, d_inner), BF16) + _nbytes((tm, d), BF16)
                + _nbytes((2 * tm + V7X_SUBLANES, conv_dim), F32) + _nbytes((tm, hp), F32)
                + _nbytes((tm + SSD_D_STATE, d_inner), F32))
    return pl.pallas_call(
        functools.partial(_ssd_kernel, col_chunk=SSD_COL_CHUNK),
        out_shape=jax.ShapeDtypeStruct((s, d), F32),
        grid=(s // tm,),
        in_specs=[_row_tile(tm, d), _resident((1, d)), _resident_layer(w_in, layer),
                  _resident((d, hp)), _resident_layer(conv_w, layer), _resident((1, conv_dim)),
                  _resident((1, hp)), _resident((1, hp)), _resident((1, d_inner)), _resident((1, d_inner)),
                  _resident_layer(w_out, layer), _resident(tri.shape), _resident(expand.shape)],
        out_specs=_row_tile(tm, d),
        scratch_shapes=scratch,
        compiler_params=pltpu.CompilerParams(dimension_semantics=("arbitrary",),
                                             vmem_limit_bytes=_vmem_limit(resident)),
        name="ssd",
    )(x, nw, w_in, wdt, conv_w, conv_b[None, :], dtb, alog, dskip, norm_w[None, :], w_out, tri, expand)


def kernel(x, norm_w, ffn_w_gate, ffn_w_up, ffn_w_down, ssd_w_in, ssd_conv_w, ssd_conv_b, ssd_dt_bias,
           ssd_a_log, ssd_d, ssd_norm_w, ssd_w_out, sc_w_in, sc_conv_w, sc_w_out, final_norm_w):
    b, s, d = x.shape
    depth = norm_w.shape[0]
    xs = x.reshape(b * s, d)
    assert b == 1, "causal state is carried across row tiles of one sequence"
    fnw = final_norm_w[None, :]
    ssd_w_in_bf16 = ssd_w_in.astype(BF16)
    for i in range(depth):
        j = i // N_MIXERS
        xs = _ffn(xs, norm_w[i, 0][None, :], ffn_w_gate, ffn_w_up, ffn_w_down, fnw, (i, 0), final_norm=False)
        if i % N_MIXERS == 0:
            xs = _ssd(xs, norm_w[i, 1][None, :], ssd_w_in_bf16, ssd_conv_w, ssd_conv_b[j], ssd_dt_bias[j],
                      ssd_a_log[j], ssd_d[j], ssd_norm_w[j], ssd_w_out, (j,))
        else:
            xs = _shortconv(xs, norm_w[i, 1][None, :], sc_w_in, sc_conv_w, sc_w_out, (j,))
        xs = _ffn(xs, norm_w[i, 2][None, :], ffn_w_gate, ffn_w_up, ffn_w_down, fnw, (i, 1),
                  final_norm=(i == depth - 1))
    return xs.reshape(b, s, d)
```

```python
import functools
import math

import jax
import jax.numpy as jnp
from jax import lax
from jax.experimental import pallas as pl
from jax.experimental.pallas import tpu as pltpu

F32 = jnp.float32
BF16 = jnp.bfloat16

RMS_EPS = 1e-5
SSD_HEAD_DIM = 64
SSD_N_GROUPS = 4
SSD_D_STATE = 128
SSD_CHUNK = 128
N_MIXERS = 2
LOG2_E = 1.4426950408889634

V7X_LANES = 128
V7X_SUBLANES = 8
V7X_VMEM_BYTES = 64 * 1024 * 1024

FFN_ROW_TILE = 512
FFN_COL_CHUNK = 256
SC_ROW_TILE = 512
SSD_ROW_TILE = 256
SSD_COL_CHUNK = 512
SPLIT_TERMS = 3


def _vmem_limit(resident_bytes):
    return int(min(2 * resident_bytes, V7X_VMEM_BYTES - 8 * 1024 * 1024))


def _nbytes(shape, dtype):
    return math.prod(shape) * jnp.dtype(dtype).itemsize


def _rmsnorm(x, w):
    inv = lax.rsqrt(jnp.mean(x * x, axis=-1, keepdims=True) + RMS_EPS)
    return (x * inv) * w


def _silu(x):
    return x * jax.nn.sigmoid(x)


def _softplus(x):
    return jnp.maximum(x, 0.0) + jnp.log1p(jnp.exp(-jnp.abs(x)))


def _dot(a, b):
    return jnp.dot(a, b, preferred_element_type=F32)


def _bf16_terms(v, n):
    terms = []
    for _ in range(n - 1):
        t = v.astype(BF16)
        terms.append(t)
        v = v - t.astype(F32)
    terms.append(v.astype(BF16))
    return terms


def _causal_conv(ext, w_ref, cols, k_w):
    t, c = ext.shape
    sub = V7X_SUBLANES
    last_sublane = lax.broadcasted_iota(jnp.int32, (t // sub, sub, c), 1) == sub - 1

    def shift_one_row(a):
        a = a.reshape(t // sub, sub, c)
        return pltpu.roll(jnp.where(last_sublane, pltpu.roll(a, 1, axis=0), a), 1, axis=1).reshape(t, c)

    acc = ext * w_ref[0:1, cols]
    for k in range(1, k_w):
        acc = ext * w_ref[k:k + 1, cols] + shift_one_row(acc)
    return acc


def _resident(shape):
    return pl.BlockSpec(shape, lambda i: (0,) * len(shape), pipeline_mode=pl.Buffered(1))


def _resident_layer(stacked, index):
    lead = len(index)
    shape = (None,) * lead + tuple(stacked.shape[lead:])
    return pl.BlockSpec(shape, lambda i: tuple(index) + (0,) * (stacked.ndim - lead),
                        pipeline_mode=pl.Buffered(1))


def _row_tile(tm, d):
    return pl.BlockSpec((tm, d), lambda i: (i, 0))


def _ffn_kernel(x_ref, nw_ref, wg_hbm, wu_hbm, wd_hbm, fnw_ref, o_ref, wg_ref, wu_ref, wd_ref, act_ref, sem, *,
                layer, col_chunk, final_norm):
    d_ff = wg_ref.shape[1]
    n_chunks = d_ff // col_chunk

    def chunk_copies(c):
        cols = pl.ds(c * col_chunk, col_chunk)
        return (pltpu.make_async_copy(wg_hbm.at[layer + (slice(None), cols)], wg_ref.at[:, cols], sem.at[0, c]),
                pltpu.make_async_copy(wu_hbm.at[layer + (slice(None), cols)], wu_ref.at[:, cols], sem.at[1, c]),
                pltpu.make_async_copy(wd_hbm.at[layer + (cols, slice(None))], wd_ref.at[cols, :], sem.at[2, c]))

    def body(wait_for_weights):
        x = x_ref[...]
        h = _rmsnorm(x, nw_ref[...]).astype(BF16)
        for c in range(n_chunks):
            cols = slice(c * col_chunk, (c + 1) * col_chunk)
            if wait_for_weights:
                gate_copy, up_copy, _ = chunk_copies(c)
                gate_copy.wait()
                up_copy.wait()
            g = _dot(h, wg_ref[:, cols].astype(BF16))
            u = _dot(h, wu_ref[:, cols].astype(BF16))
            act_ref[:, cols] = (_silu(g) * u).astype(BF16)
        if wait_for_weights:
            for c in range(n_chunks):
                chunk_copies(c)[2].wait()
        out = x + 0.5 * _dot(act_ref[...], wd_ref[...].astype(BF16))
        if final_norm:
            out = _rmsnorm(out, fnw_ref[...])
        o_ref[...] = out

    @pl.when(pl.program_id(0) == 0)
    def _():
        for c in range(n_chunks):
            gate_copy, up_copy, _ = chunk_copies(c)
            gate_copy.start()
            up_copy.start()
        for c in range(n_chunks):
            chunk_copies(c)[2].start()
        body(True)

    @pl.when(pl.program_id(0) > 0)
    def _():
        body(False)


def _ffn(x, nw, wg, wu, wd, fnw, layer, *, final_norm):
    s, d = x.shape
    d_ff = wg.shape[-1]
    tm = FFN_ROW_TILE
    assert s % tm == 0 and d_ff % FFN_COL_CHUNK == 0
    resident = (4 * _nbytes((tm, d), F32) + 3 * _nbytes((d, d_ff), F32) + _nbytes((tm, d_ff), BF16))
    in_hbm = pl.BlockSpec(memory_space=pl.ANY)
    return pl.pallas_call(
        functools.partial(_ffn_kernel, layer=tuple(layer), col_chunk=FFN_COL_CHUNK, final_norm=final_norm),
        out_shape=jax.ShapeDtypeStruct((s, d), F32),
        grid=(s // tm,),
        in_specs=[_row_tile(tm, d), _resident((1, d)), in_hbm, in_hbm, in_hbm, _resident((1, d))],
        out_specs=_row_tile(tm, d),
        scratch_shapes=[pltpu.VMEM((d, d_ff), F32), pltpu.VMEM((d, d_ff), F32), pltpu.VMEM((d_ff, d), F32),
                        pltpu.VMEM((tm, d_ff), BF16), pltpu.SemaphoreType.DMA((3, d_ff // FFN_COL_CHUNK))],
        compiler_params=pltpu.CompilerParams(dimension_semantics=("arbitrary",),
                                             vmem_limit_bytes=_vmem_limit(resident)),
        name="ffn",
    )(x, nw, wg, wu, wd, fnw)


def _sc_kernel(x_ref, nw_ref, win_ref, cw_ref, wout_ref, o_ref, cu_ref):
    tm, d = x_ref.shape
    k_w = cw_ref.shape[0]
    head = V7X_SUBLANES

    @pl.when(pl.program_id(0) == 0)
    def _():
        cu_ref[0:head, :] = jnp.zeros((head, d), F32)

    x = x_ref[...]
    h = _rmsnorm(x, nw_ref[...]).astype(BF16)
    cg = _dot(h, win_ref[:, d:2 * d].astype(BF16))
    u = _dot(h, win_ref[:, 2 * d:3 * d].astype(BF16))
    cu_ref[head:head + tm, :] = cg * u
    v = _causal_conv(cu_ref[...], cw_ref, slice(0, d), k_w)[head:head + tm]
    cu_ref[0:head, :] = cu_ref[tm:tm + head, :]
    bg = _dot(h, win_ref[:, 0:d].astype(BF16))
    o_ref[...] = x + _dot((bg * v).astype(BF16), wout_ref[...].astype(BF16))


def _shortconv(x, nw, win, cw, wout, layer):
    s, d = x.shape
    tm = SC_ROW_TILE
    assert s % tm == 0
    resident = (4 * _nbytes((tm, d), F32) + _nbytes((d, 3 * d), F32) + _nbytes((d, d), F32)
                + _nbytes((tm + V7X_SUBLANES, d), F32))
    return pl.pallas_call(
        _sc_kernel,
        out_shape=jax.ShapeDtypeStruct((s, d), F32),
        grid=(s // tm,),
        in_specs=[_row_tile(tm, d), _resident((1, d)), _resident_layer(win, layer), _resident_layer(cw, layer),
                  _resident_layer(wout, layer)],
        out_specs=_row_tile(tm, d),
        scratch_shapes=[pltpu.VMEM((tm + V7X_SUBLANES, d), F32)],
        compiler_params=pltpu.CompilerParams(dimension_semantics=("arbitrary",),
                                             vmem_limit_bytes=_vmem_limit(resident)),
        name="shortconv",
    )(x, nw, win, cw, wout)


def _ssd_chunk_decays(rows, dt_ref, alog_ref, tri_ref, expand_ref):
    chunk = SSD_CHUNK
    dt = dt_ref[rows, :]
    ac = dt * (-jnp.exp(alog_ref[...]))
    a_cs = _dot(tri_ref[...], jnp.concatenate(_bf16_terms(ac, SPLIT_TERMS), axis=0))
    a2 = a_cs * LOG2_E
    s_side_t = a2.T - jnp.log2(dt.T)
    tot = a_cs[chunk - 1:chunk, :]
    per_head = jnp.concatenate(
        [jnp.exp(tot - a_cs) * dt, jnp.exp(a_cs), jnp.broadcast_to(jnp.exp(tot), (V7X_SUBLANES, tot.shape[1]))],
        axis=0)
    per_col = _dot(jnp.concatenate(_bf16_terms(per_head, 2), axis=1), expand_ref[...])
    return a2, s_side_t, per_col[0:chunk], per_col[chunk:2 * chunk], per_col[2 * chunk:2 * chunk + 1]


def _ssd_chunk_group(g, rows, decays, xbc_ref, y_ref, state_ref, dskip_ref, *, d_inner):
    a2, s_side_t, w_state, w_off, w_tot = decays
    chunk = SSD_CHUNK
    n_state = state_ref.shape[0]
    n_groups = (xbc_ref.shape[1] - d_inner) // (2 * n_state)
    group_w = d_inner // n_groups
    pair_w = 2 * SSD_HEAD_DIM
    causal = (lax.broadcasted_iota(jnp.int32, (chunk, chunk), 0)
              >= lax.broadcasted_iota(jnp.int32, (chunk, chunk), 1))
    first_head = lax.broadcasted_iota(jnp.int32, (chunk, pair_w), 1) < SSD_HEAD_DIM

    gcols = slice(g * group_w, (g + 1) * group_w)
    xg = xbc_ref[rows, gcols]
    bg = xbc_ref[rows, d_inner + g * n_state:d_inner + (g + 1) * n_state].astype(BF16)
    cg = xbc_ref[rows, d_inner + (n_groups + g) * n_state:d_inner + (n_groups + g + 1) * n_state].astype(BF16)
    cb = lax.dot_general(cg, bg, (((1,), (1,)), ((), ())), preferred_element_type=F32)
    s_old = state_ref[:, gcols]
    y_off = _dot(cg, s_old.astype(BF16)) * w_off[:, gcols]
    wx = (w_state[:, gcols] * xg).astype(BF16)
    state_ref[:, gcols] = w_tot[:, gcols] * s_old + lax.dot_general(
        bg, wx, (((0,), (0,)), ((), ())), preferred_element_type=F32)
    for j in range(group_w // pair_w):
        h0 = (g * group_w + j * pair_w) // SSD_HEAD_DIM
        pcols = slice(g * group_w + j * pair_w, g * group_w + (j + 1) * pair_w)
        m = []
        for hh in (h0, h0 + 1):
            seg2 = a2[:, hh:hh + 1] - s_side_t[hh:hh + 1, :]
            m.append(cb * jnp.exp2(jnp.where(causal, seg2, -jnp.inf)))
        lhs = jnp.concatenate(m, axis=1).astype(BF16)
        xp = xg[:, j * pair_w:(j + 1) * pair_w]
        rhs = jnp.concatenate([jnp.where(first_head, xp, 0.0), jnp.where(first_head, 0.0, xp)],
                              axis=0).astype(BF16)
        y_ref[rows, pcols] = (_dot(lhs, rhs) + y_off[:, j * pair_w:(j + 1) * pair_w]
                              + xp * dskip_ref[:, pcols])


def _ssd_kernel(x_ref, nw_ref, win_ref, wdt_ref, cw_ref, cb_ref, dtb_ref, alog_ref, dskip_ref, gnw_ref,
                wout_ref, tri_ref, expand_ref,
                o_ref,
                h_ref, conv_ref, xbc_ref, dt_ref, y_ref, state_ref, *, col_chunk):
    tm, d = x_ref.shape
    d_inner = wout_ref.shape[0]
    conv_dim = cw_ref.shape[1]
    k_w = cw_ref.shape[0]
    n_groups = (conv_dim - d_inner) // (2 * state_ref.shape[0])
    group_w = d_inner // n_groups
    head = V7X_SUBLANES

    @pl.when(pl.program_id(0) == 0)
    def _():
        conv_ref[0:head, :] = jnp.zeros((head, conv_dim), F32)
        state_ref[...] = jnp.zeros(state_ref.shape, F32)

    x = x_ref[...]
    h = _rmsnorm(x, nw_ref[...]).astype(BF16)
    h_ref[...] = h
    for c in range(conv_dim // col_chunk):
        cols = slice(c * col_chunk, (c + 1) * col_chunk)
        conv_ref[head:head + tm, cols] = _dot(
            h, win_ref[:, d_inner + c * col_chunk:d_inner + (c + 1) * col_chunk])
        acc = _causal_conv(conv_ref[:, cols], cw_ref, cols, k_w)[head:head + tm] + cb_ref[:, cols]
        xbc_ref[:, cols] = _silu(acc)
    conv_ref[0:head, :] = conv_ref[tm:tm + head, :]
    dt_ref[...] = _softplus(_dot(h, wdt_ref[...]) + dtb_ref[...])

    for c in range(tm // SSD_CHUNK):
        rows = slice(c * SSD_CHUNK, (c + 1) * SSD_CHUNK)
        decays = _ssd_chunk_decays(rows, dt_ref, alog_ref, tri_ref, expand_ref)
        for g in range(n_groups):
            _ssd_chunk_group(g, rows, decays, xbc_ref, y_ref, state_ref, dskip_ref, d_inner=d_inner)

    gated = y_ref[...] * _silu(_dot(h_ref[...], win_ref[:, 0:d_inner]))
    parts = []
    for g in range(n_groups):
        gg = gated[:, g * group_w:(g + 1) * group_w]
        parts.append(gg * lax.rsqrt(jnp.mean(gg * gg, axis=-1, keepdims=True) + RMS_EPS))
    normed = (jnp.concatenate(parts, axis=1) * gnw_ref[...]).astype(BF16)
    o_ref[...] = x + _dot(normed, wout_ref[...].astype(BF16))


def _ssd(x, nw, w_in, conv_w, conv_b, dt_bias, a_log, d_skip, norm_w, w_out, layer):
    s, d = x.shape
    d_inner = w_out.shape[-2]
    conv_dim = conv_w.shape[-1]
    n_heads = dt_bias.shape[0]
    hp = V7X_LANES
    assert n_heads <= hp and n_heads * SSD_HEAD_DIM == d_inner
    tm = SSD_ROW_TILE
    assert s % tm == 0 and tm % SSD_CHUNK == 0 and conv_dim % SSD_COL_CHUNK == 0

    def pad_heads(v):
        return jnp.pad(v, [(0, 0)] * (v.ndim - 1) + [(0, hp - n_heads)])

    wdt = pad_heads(w_in[layer][:, d_inner + conv_dim:])
    dtb = pad_heads(dt_bias[None, :])
    alog = pad_heads(a_log[None, :])
    dskip = jnp.repeat(d_skip, SSD_HEAD_DIM)[None, :]
    tri = jnp.tile(jnp.tril(jnp.ones((SSD_CHUNK, SSD_CHUNK), BF16)), (1, SPLIT_TERMS))
    expand = jnp.tile((jnp.arange(hp)[:, None] == (jnp.arange(d_inner) // SSD_HEAD_DIM)[None, :]).astype(BF16),
                      (2, 1))

    scratch = [
        pltpu.VMEM((tm, d), BF16),
        pltpu.VMEM((tm + V7X_SUBLANES, conv_dim), F32),
        pltpu.VMEM((tm, conv_dim), F32),
        pltpu.VMEM((tm, hp), F32),
        pltpu.VMEM((tm, d_inner), F32),
        pltpu.VMEM((SSD_D_STATE, d_inner), F32),
    ]
    resident = (4 * _nbytes((tm, d), F32) + _nbytes(w_in.shape[1:], BF16) + _nbytes(w_out.shape[1:], F32)
                + _nbytes((d, hp), BF16) + _nbytes((2 * hp, d_inner), BF16) + _nbytes((tm, d), BF16)
                + _nbytes((2 * tm + V7X_SUBLANES, conv_dim), F32) + _nbytes((tm, hp), F32)
                + _nbytes((tm + SSD_D_STATE, d_inner), F32))
    return pl.pallas_call(
        functools.partial(_ssd_kernel, col_chunk=SSD_COL_CHUNK),
        out_shape=jax.ShapeDtypeStruct((s, d), F32),
        grid=(s // tm,),
        in_specs=[_row_tile(tm, d), _resident((1, d)), _resident_layer(w_in, layer),
                  _resident((d, hp)), _resident_layer(conv_w, layer), _resident((1, conv_dim)),
                  _resident((1, hp)), _resident((1, hp)), _resident((1, d_inner)), _resident((1, d_inner)),
                  _resident_layer(w_out, layer), _resident(tri.shape), _resident(expand.shape)],
        out_specs=_row_tile(tm, d),
        scratch_shapes=scratch,
        compiler_params=pltpu.CompilerParams(dimension_semantics=("arbitrary",),
                                             vmem_limit_bytes=_vmem_limit(resident)),
        name="ssd",
    )(x, nw, w_in, wdt, conv_w, conv_b[None, :], dtb, alog, dskip, norm_w[None, :], w_out, tri, expand)


def kernel(x, norm_w, ffn_w_gate, ffn_w_up, ffn_w_down, ssd_w_in, ssd_conv_w, ssd_conv_b, ssd_dt_bias,
           ssd_a_log, ssd_d, ssd_norm_w, ssd_w_out, sc_w_in, sc_conv_w, sc_w_out, final_norm_w):
    b, s, d = x.shape
    depth = norm_w.shape[0]
    xs = x.reshape(b * s, d)
    assert b == 1, "causal state is carried across row tiles of one sequence"
    fnw = final_norm_w[None, :]
    ssd_w_in_bf16 = ssd_w_in.astype(BF16)
    for i in range(depth):
        j = i // N_MIXERS
        xs = _ffn(xs, norm_w[i, 0][None, :], ffn_w_gate, ffn_w_up, ffn_w_down, fnw, (i, 0), final_norm=False)
        if i % N_MIXERS == 0:
            xs = _ssd(xs, norm_w[i, 1][None, :], ssd_w_in_bf16, ssd_conv_w, ssd_conv_b[j], ssd_dt_bias[j],
                      ssd_a_log[j], ssd_d[j], ssd_norm_w[j], ssd_w_out, (j,))
        else:
            xs = _shortconv(xs, norm_w[i, 1][None, :], sc_w_in, sc_conv_w, sc_w_out, (j,))
        xs = _ffn(xs, norm_w[i, 2][None, :], ffn_w_gate, ffn_w_up, ffn_w_down, fnw, (i, 1),
                  final_norm=(i == depth - 1))
    return xs.reshape(b, s, d)
```

```python
import functools
import math

import jax
import jax.numpy as jnp
from jax import lax
from jax.experimental import pallas as pl
from jax.experimental.pallas import tpu as pltpu

F32 = jnp.float32
BF16 = jnp.bfloat16

RMS_EPS = 1e-5
SSD_HEAD_DIM = 64
SSD_N_GROUPS = 4
SSD_D_STATE = 128
SSD_CHUNK = 128
N_MIXERS = 2
LOG2_E = 1.4426950408889634

V7X_LANES = 128
V7X_SUBLANES = 8
V7X_VMEM_BYTES = 64 * 1024 * 1024

FFN_ROW_TILE = 1024
FFN_COL_CHUNK = 256
FFN_COL_STAGE_SLOTS = 4
FFN_ROW_STAGE_SLOTS = 2
SC_ROW_TILE = 1024
SSD_ROW_TILE = 512
SSD_COL_CHUNK = 512
SPLIT_TERMS = 3


def _vmem_limit(resident_bytes):
    return int(min(2 * resident_bytes, V7X_VMEM_BYTES - 8 * 1024 * 1024))


def _nbytes(shape, dtype):
    return math.prod(shape) * jnp.dtype(dtype).itemsize


def _rmsnorm(x, w):
    inv = lax.rsqrt(jnp.mean(x * x, axis=-1, keepdims=True) + RMS_EPS)
    return (x * inv) * w


def _silu(x):
    return x * jax.nn.sigmoid(x)


def _softplus(x):
    return jnp.maximum(x, 0.0) + jnp.log1p(jnp.exp(-jnp.abs(x)))


def _dot(a, b):
    return jnp.dot(a, b, preferred_element_type=F32)


def _bf16_terms(v, n):
    terms = []
    for _ in range(n - 1):
        t = v.astype(BF16)
        terms.append(t)
        v = v - t.astype(F32)
    terms.append(v.astype(BF16))
    return terms


def _causal_conv(ext, w_ref, cols, k_w):
    t, c = ext.shape
    sub = V7X_SUBLANES
    last_sublane = lax.broadcasted_iota(jnp.int32, (t // sub, sub, c), 1) == sub - 1

    def shift_one_row(a):
        a = a.reshape(t // sub, sub, c)
        return pltpu.roll(jnp.where(last_sublane, pltpu.roll(a, 1, axis=0), a), 1, axis=1).reshape(t, c)

    acc = ext * w_ref[0:1, cols]
    for k in range(1, k_w):
        acc = ext * w_ref[k:k + 1, cols] + shift_one_row(acc)
    return acc


def _resident(shape):
    return pl.BlockSpec(shape, lambda i: (0,) * len(shape), pipeline_mode=pl.Buffered(1))


def _resident_layer(stacked, index):
    lead = len(index)
    shape = (None,) * lead + tuple(stacked.shape[lead:])
    return pl.BlockSpec(shape, lambda i: tuple(index) + (0,) * (stacked.ndim - lead),
                        pipeline_mode=pl.Buffered(1))


def _row_tile(tm, d):
    return pl.BlockSpec((tm, d), lambda i: (i, 0))


def _ffn_kernel(x_ref, nw_ref, wg_hbm, wu_hbm, wd_hbm, fnw_ref, o_ref,
                wg_ref, wu_ref, wd_ref, act_ref, stage_col, stage_row, sem, *, layer, col_chunk, final_norm):
    d_ff = wg_ref.shape[1]
    n_chunks = d_ff // col_chunk
    n_col_slots = stage_col.shape[0]
    n_row_slots = stage_row.shape[0]

    def col_copy(q):
        c, is_up = divmod(q, 2)
        src = (wu_hbm if is_up else wg_hbm).at[layer + (slice(None), pl.ds(c * col_chunk, col_chunk))]
        return pltpu.make_async_copy(src, stage_col.at[q % n_col_slots], sem.at[is_up, c])

    def row_copy(c):
        src = wd_hbm.at[layer + (pl.ds(c * col_chunk, col_chunk), slice(None))]
        return pltpu.make_async_copy(src, stage_row.at[c % n_row_slots], sem.at[2, c])

    def stream_chunk(c):
        cols = slice(c * col_chunk, (c + 1) * col_chunk)
        col_copy(2 * c).wait()
        col_copy(2 * c + 1).wait()
        row_copy(c).wait()
        wg_ref[:, cols] = stage_col[(2 * c) % n_col_slots].astype(BF16)
        wu_ref[:, cols] = stage_col[(2 * c + 1) % n_col_slots].astype(BF16)
        wd_ref[cols, :] = stage_row[c % n_row_slots].astype(BF16)
        for q in (2 * c + n_col_slots, 2 * c + 1 + n_col_slots):
            if q < 2 * n_chunks:
                col_copy(q).start()
        if c + n_row_slots < n_chunks:
            row_copy(c + n_row_slots).start()

    def body(stream_weights):
        x = x_ref[...]
        h = _rmsnorm(x, nw_ref[...]).astype(BF16)
        for c in range(n_chunks):
            cols = slice(c * col_chunk, (c + 1) * col_chunk)
            if stream_weights:
                stream_chunk(c)
            g = _dot(h, wg_ref[:, cols])
            u = _dot(h, wu_ref[:, cols])
            act_ref[:, cols] = (_silu(g) * u).astype(BF16)
        out = x + 0.5 * _dot(act_ref[...], wd_ref[...])
        if final_norm:
            out = _rmsnorm(out, fnw_ref[...])
        o_ref[...] = out

    @pl.when(pl.program_id(0) == 0)
    def _():
        for q in range(min(n_col_slots, 2 * n_chunks)):
            col_copy(q).start()
        for c in range(min(n_row_slots, n_chunks)):
            row_copy(c).start()
        body(True)

    @pl.when(pl.program_id(0) > 0)
    def _():
        body(False)


def _ffn(x, nw, wg, wu, wd, fnw, layer, *, final_norm):
    s, d = x.shape
    d_ff = wg.shape[-1]
    tm, cc = FFN_ROW_TILE, FFN_COL_CHUNK
    assert s % tm == 0 and d_ff % cc == 0 and FFN_COL_STAGE_SLOTS % 2 == 0
    resident = (4 * _nbytes((tm, d), F32) + 3 * _nbytes((d, d_ff), BF16) + _nbytes((tm, d_ff), BF16)
                + (FFN_COL_STAGE_SLOTS + FFN_ROW_STAGE_SLOTS) * _nbytes((d, cc), F32))
    in_hbm = pl.BlockSpec(memory_space=pl.ANY)
    return pl.pallas_call(
        functools.partial(_ffn_kernel, layer=tuple(layer), col_chunk=cc, final_norm=final_norm),
        out_shape=jax.ShapeDtypeStruct((s, d), F32),
        grid=(s // tm,),
        in_specs=[_row_tile(tm, d), _resident((1, d)), in_hbm, in_hbm, in_hbm, _resident((1, d))],
        out_specs=_row_tile(tm, d),
        scratch_shapes=[pltpu.VMEM((d, d_ff), BF16), pltpu.VMEM((d, d_ff), BF16), pltpu.VMEM((d_ff, d), BF16),
                        pltpu.VMEM((tm, d_ff), BF16),
                        pltpu.VMEM((FFN_COL_STAGE_SLOTS, d, cc), F32), pltpu.VMEM((FFN_ROW_STAGE_SLOTS, cc, d), F32),
                        pltpu.SemaphoreType.DMA((3, d_ff // cc))],
        compiler_params=pltpu.CompilerParams(dimension_semantics=("arbitrary",),
                                             vmem_limit_bytes=_vmem_limit(resident)),
        name="ffn",
    )(x, nw, wg, wu, wd, fnw)


def _sc_kernel(x_ref, nw_ref, win_ref, cw_ref, wout_ref, o_ref, cu_ref):
    tm, d = x_ref.shape
    k_w = cw_ref.shape[0]
    head = V7X_SUBLANES

    @pl.when(pl.program_id(0) == 0)
    def _():
        cu_ref[0:head, :] = jnp.zeros((head, d), F32)

    x = x_ref[...]
    h = _rmsnorm(x, nw_ref[...]).astype(BF16)
    cg = _dot(h, win_ref[:, d:2 * d].astype(BF16))
    u = _dot(h, win_ref[:, 2 * d:3 * d].astype(BF16))
    cu_ref[head:head + tm, :] = cg * u
    v = _causal_conv(cu_ref[...], cw_ref, slice(0, d), k_w)[head:head + tm]
    cu_ref[0:head, :] = cu_ref[tm:tm + head, :]
    bg = _dot(h, win_ref[:, 0:d].astype(BF16))
    o_ref[...] = x + _dot((bg * v).astype(BF16), wout_ref[...].astype(BF16))


def _shortconv(x, nw, win, cw, wout, layer):
    s, d = x.shape
    tm = SC_ROW_TILE
    assert s % tm == 0
    resident = (4 * _nbytes((tm, d), F32) + _nbytes((d, 3 * d), F32) + _nbytes((d, d), F32)
                + _nbytes((tm + V7X_SUBLANES, d), F32))
    return pl.pallas_call(
        _sc_kernel,
        out_shape=jax.ShapeDtypeStruct((s, d), F32),
        grid=(s // tm,),
        in_specs=[_row_tile(tm, d), _resident((1, d)), _resident_layer(win, layer), _resident_layer(cw, layer),
                  _resident_layer(wout, layer)],
        out_specs=_row_tile(tm, d),
        scratch_shapes=[pltpu.VMEM((tm + V7X_SUBLANES, d), F32)],
        compiler_params=pltpu.CompilerParams(dimension_semantics=("arbitrary",),
                                             vmem_limit_bytes=_vmem_limit(resident)),
        name="shortconv",
    )(x, nw, win, cw, wout)


def _ssd_chunk_decays(rows, dt_ref, alog_ref, tri_ref, expand_ref):
    chunk = SSD_CHUNK
    dt = dt_ref[rows, :]
    ac = dt * (-jnp.exp(alog_ref[...]))
    a_cs = _dot(tri_ref[...], jnp.concatenate(_bf16_terms(ac, SPLIT_TERMS), axis=0))
    a2 = a_cs * LOG2_E
    s_side_t = a2.T - jnp.log2(dt.T)
    tot = a_cs[chunk - 1:chunk, :]
    per_head = jnp.concatenate(
        [jnp.exp(tot - a_cs) * dt, jnp.exp(a_cs), jnp.broadcast_to(jnp.exp(tot), (V7X_SUBLANES, tot.shape[1]))],
        axis=0)
    per_col = _dot(jnp.concatenate(_bf16_terms(per_head, 2), axis=1), expand_ref[...])
    return a2, s_side_t, per_col[0:chunk], per_col[chunk:2 * chunk], per_col[2 * chunk:2 * chunk + 1]


def _ssd_chunk_group(g, rows, decays, xbc_ref, y_ref, state_ref, dskip_ref, *, d_inner):
    a2, s_side_t, w_state, w_off, w_tot = decays
    chunk = SSD_CHUNK
    n_state = state_ref.shape[0]
    n_groups = (xbc_ref.shape[1] - d_inner) // (2 * n_state)
    group_w = d_inner // n_groups
    pair_w = 2 * SSD_HEAD_DIM
    causal = (lax.broadcasted_iota(jnp.int32, (chunk, chunk), 0)
              >= lax.broadcasted_iota(jnp.int32, (chunk, chunk), 1))
    first_head = lax.broadcasted_iota(jnp.int32, (chunk, pair_w), 1) < SSD_HEAD_DIM

    gcols = slice(g * group_w, (g + 1) * group_w)
    xg = xbc_ref[rows, gcols]
    bg = xbc_ref[rows, d_inner + g * n_state:d_inner + (g + 1) * n_state].astype(BF16)
    cg = xbc_ref[rows, d_inner + (n_groups + g) * n_state:d_inner + (n_groups + g + 1) * n_state].astype(BF16)
    cb = lax.dot_general(cg, bg, (((1,), (1,)), ((), ())), preferred_element_type=F32)
    s_old = state_ref[:, gcols]
    y_off = _dot(cg, s_old.astype(BF16)) * w_off[:, gcols]
    wx = (w_state[:, gcols] * xg).astype(BF16)
    state_ref[:, gcols] = w_tot[:, gcols] * s_old + lax.dot_general(
        bg, wx, (((0,), (0,)), ((), ())), preferred_element_type=F32)
    for j in range(group_w // pair_w):
        h0 = (g * group_w + j * pair_w) // SSD_HEAD_DIM
        pcols = slice(g * group_w + j * pair_w, g * group_w + (j + 1) * pair_w)
        m = []
        for hh in (h0, h0 + 1):
            seg2 = a2[:, hh:hh + 1] - s_side_t[hh:hh + 1, :]
            m.append(cb * jnp.exp2(jnp.where(causal, seg2, -jnp.inf)))
        lhs = jnp.concatenate(m, axis=1).astype(BF16)
        xp = xg[:, j * pair_w:(j + 1) * pair_w]
        rhs = jnp.concatenate([jnp.where(first_head, xp, 0.0), jnp.where(first_head, 0.0, xp)],
                              axis=0).astype(BF16)
        y_ref[rows, pcols] = (_dot(lhs, rhs) + y_off[:, j * pair_w:(j + 1) * pair_w]
                              + xp * dskip_ref[:, pcols])


def _ssd_kernel(x_ref, nw_ref, win_ref, wdt_ref, cw_ref, cb_ref, dtb_ref, alog_ref, dskip_ref, gnw_ref,
                wout_ref, tri_ref, expand_ref,
                o_ref,
                h_ref, conv_ref, xbc_ref, dt_ref, y_ref, state_ref, *, col_chunk):
    tm, d = x_ref.shape
    d_inner = wout_ref.shape[0]
    conv_dim = cw_ref.shape[1]
    k_w = cw_ref.shape[0]
    n_groups = (conv_dim - d_inner) // (2 * state_ref.shape[0])
    group_w = d_inner // n_groups
    head = V7X_SUBLANES

    @pl.when(pl.program_id(0) == 0)
    def _():
        conv_ref[0:head, :] = jnp.zeros((head, conv_dim), F32)
        state_ref[...] = jnp.zeros(state_ref.shape, F32)

    x = x_ref[...]
    h = _rmsnorm(x, nw_ref[...]).astype(BF16)
    h_ref[...] = h
    for c in range(conv_dim // col_chunk):
        cols = slice(c * col_chunk, (c + 1) * col_chunk)
        conv_ref[head:head + tm, cols] = _dot(
            h, win_ref[:, d_inner + c * col_chunk:d_inner + (c + 1) * col_chunk])
        acc = _causal_conv(conv_ref[:, cols], cw_ref, cols, k_w)[head:head + tm] + cb_ref[:, cols]
        xbc_ref[:, cols] = _silu(acc)
    conv_ref[0:head, :] = conv_ref[tm:tm + head, :]
    dt_ref[...] = _softplus(_dot(h, wdt_ref[...]) + dtb_ref[...])

    for c in range(tm // SSD_CHUNK):
        rows = slice(c * SSD_CHUNK, (c + 1) * SSD_CHUNK)
        decays = _ssd_chunk_decays(rows, dt_ref, alog_ref, tri_ref, expand_ref)
        for g in range(n_groups):
            _ssd_chunk_group(g, rows, decays, xbc_ref, y_ref, state_ref, dskip_ref, d_inner=d_inner)

    gated = y_ref[...] * _silu(_dot(h_ref[...], win_ref[:, 0:d_inner]))
    parts = []
    for g in range(n_groups):
        gg = gated[:, g * group_w:(g + 1) * group_w]
        parts.append(gg * lax.rsqrt(jnp.mean(gg * gg, axis=-1, keepdims=True) + RMS_EPS))
    normed = (jnp.concatenate(parts, axis=1) * gnw_ref[...]).astype(BF16)
    o_ref[...] = x + _dot(normed, wout_ref[...].astype(BF16))


def _ssd(x, nw, w_in, conv_w, conv_b, dt_bias, a_log, d_skip, norm_w, w_out, layer):
    s, d = x.shape
    d_inner = w_out.shape[-2]
    conv_dim = conv_w.shape[-1]
    n_heads = dt_bias.shape[0]
    hp = V7X_LANES
    assert n_heads <= hp and n_heads * SSD_HEAD_DIM == d_inner
    tm = SSD_ROW_TILE
    assert s % tm == 0 and tm % SSD_CHUNK == 0 and conv_dim % SSD_COL_CHUNK == 0

    def pad_heads(v):
        return jnp.pad(v, [(0, 0)] * (v.ndim - 1) + [(0, hp - n_heads)])

    wdt = pad_heads(w_in[layer][:, d_inner + conv_dim:])
    dtb = pad_heads(dt_bias[None, :])
    alog = pad_heads(a_log[None, :])
    dskip = jnp.repeat(d_skip, SSD_HEAD_DIM)[None, :]
    tri = jnp.tile(jnp.tril(jnp.ones((SSD_CHUNK, SSD_CHUNK), BF16)), (1, SPLIT_TERMS))
    expand = jnp.tile((jnp.arange(hp)[:, None] == (jnp.arange(d_inner) // SSD_HEAD_DIM)[None, :]).astype(BF16),
                      (2, 1))

    scratch = [
        pltpu.VMEM((tm, d), BF16),
        pltpu.VMEM((tm + V7X_SUBLANES, conv_dim), F32),
        pltpu.VMEM((tm, conv_dim), F32),
        pltpu.VMEM((tm, hp), F32),
        pltpu.VMEM((tm, d_inner), F32),
        pltpu.VMEM((SSD_D_STATE, d_inner), F32),
    ]
    resident = (4 * _nbytes((tm, d), F32) + _nbytes(w_in.shape[1:], BF16) + _nbytes(w_out.shape[1:], F32)
                + _nbytes((d, hp), BF16) + _nbytes((2 * hp, d_inner), BF16) + _nbytes((tm, d), BF16)
                + _nbytes((2 * tm + V7X_SUBLANES, conv_dim), F32) + _nbytes((tm, hp), F32)
                + _nbytes((tm + SSD_D_STATE, d_inner), F32))
    return pl.pallas_call(
        functools.partial(_ssd_kernel, col_chunk=SSD_COL_CHUNK),
        out_shape=jax.ShapeDtypeStruct((s, d), F32),
        grid=(s // tm,),
        in_specs=[_row_tile(tm, d), _resident((1, d)), _resident_layer(w_in, layer),
                  _resident((d, hp)), _resident_layer(conv_w, layer), _resident((1, conv_dim)),
                  _resident((1, hp)), _resident((1, hp)), _resident((1, d_inner)), _resident((1, d_inner)),
                  _resident_layer(w_out, layer), _resident(tri.shape), _resident(expand.shape)],
        out_specs=_row_tile(tm, d),
        scratch_shapes=scratch,
        compiler_params=pltpu.CompilerParams(dimension_semantics=("arbitrary",),
                                             vmem_limit_bytes=_vmem_limit(resident)),
        name="ssd",
    )(x, nw, w_in, wdt, conv_w, conv_b[None, :], dtb, alog, dskip, norm_w[None, :], w_out, tri, expand)


def kernel(x, norm_w, ffn_w_gate, ffn_w_up, ffn_w_down, ssd_w_in, ssd_conv_w, ssd_conv_b, ssd_dt_bias,
           ssd_a_log, ssd_d, ssd_norm_w, ssd_w_out, sc_w_in, sc_conv_w, sc_w_out, final_norm_w):
    b, s, d = x.shape
    depth = norm_w.shape[0]
    xs = x.reshape(b * s, d)
    assert b == 1, "causal state is carried across row tiles of one sequence"
    fnw = final_norm_w[None, :]
    ssd_w_in_bf16 = ssd_w_in.astype(BF16)
    for i in range(depth):
        j = i // N_MIXERS
        xs = _ffn(xs, norm_w[i, 0][None, :], ffn_w_gate, ffn_w_up, ffn_w_down, fnw, (i, 0), final_norm=False)
        if i % N_MIXERS == 0:
            xs = _ssd(xs, norm_w[i, 1][None, :], ssd_w_in_bf16, ssd_conv_w, ssd_conv_b[j], ssd_dt_bias[j],
                      ssd_a_log[j], ssd_d[j], ssd_norm_w[j], ssd_w_out, (j,))
        else:
            xs = _shortconv(xs, norm_w[i, 1][None, :], sc_w_in, sc_conv_w, sc_w_out, (j,))
        xs = _ffn(xs, norm_w[i, 2][None, :], ffn_w_gate, ffn_w_up, ffn_w_down, fnw, (i, 1),
                  final_norm=(i == depth - 1))
    return xs.reshape(b, s, d)
```

```python
import functools
import math

import jax
import jax.numpy as jnp
from jax import lax
from jax.experimental import pallas as pl
from jax.experimental.pallas import tpu as pltpu

F32 = jnp.float32
BF16 = jnp.bfloat16

RMS_EPS = 1e-5
SSD_HEAD_DIM = 64
SSD_N_GROUPS = 4
SSD_D_STATE = 128
SSD_CHUNK = 128
N_MIXERS = 2
LOG2_E = 1.4426950408889634

V7X_LANES = 128
V7X_SUBLANES = 8
V7X_VMEM_BYTES = 64 * 1024 * 1024

FFN_ROW_TILE = 512
FFN_COL_CHUNK = 256
SC_ROW_TILE = 1024
SSD_ROW_TILE = 512
SSD_COL_CHUNK = 512
SPLIT_TERMS = 3


def _vmem_limit(resident_bytes):
    return int(min(2 * resident_bytes, V7X_VMEM_BYTES - 8 * 1024 * 1024))


def _nbytes(shape, dtype):
    return math.prod(shape) * jnp.dtype(dtype).itemsize


def _rmsnorm(x, w):
    inv = lax.rsqrt(jnp.mean(x * x, axis=-1, keepdims=True) + RMS_EPS)
    return (x * inv) * w


def _silu(x):
    return x * jax.nn.sigmoid(x)


def _softplus(x):
    return jnp.maximum(x, 0.0) + jnp.log1p(jnp.exp(-jnp.abs(x)))


def _dot(a, b):
    return jnp.dot(a, b, preferred_element_type=F32)


def _bf16_terms(v, n):
    terms = []
    for _ in range(n - 1):
        t = v.astype(BF16)
        terms.append(t)
        v = v - t.astype(F32)
    terms.append(v.astype(BF16))
    return terms


def _causal_conv(ext, w_ref, cols, k_w):
    t, c = ext.shape
    sub = V7X_SUBLANES
    last_sublane = lax.broadcasted_iota(jnp.int32, (t // sub, sub, c), 1) == sub - 1

    def shift_one_row(a):
        a = a.reshape(t // sub, sub, c)
        return pltpu.roll(jnp.where(last_sublane, pltpu.roll(a, 1, axis=0), a), 1, axis=1).reshape(t, c)

    acc = ext * w_ref[0:1, cols]
    for k in range(1, k_w):
        acc = ext * w_ref[k:k + 1, cols] + shift_one_row(acc)
    return acc


def _resident(shape):
    return pl.BlockSpec(shape, lambda i: (0,) * len(shape), pipeline_mode=pl.Buffered(1))


def _resident_layer(stacked, index):
    lead = len(index)
    shape = (None,) * lead + tuple(stacked.shape[lead:])
    return pl.BlockSpec(shape, lambda i: tuple(index) + (0,) * (stacked.ndim - lead),
                        pipeline_mode=pl.Buffered(1))


def _row_tile(tm, d):
    return pl.BlockSpec((tm, d), lambda i: (i, 0))


def _ffn_kernel(x_ref, nw_ref, wg_hbm, wu_hbm, wd_hbm, fnw_ref, o_ref, wg_ref, wu_ref, wd_ref, act_ref, sem, *,
                layer, col_chunk, final_norm):
    d_ff = wg_ref.shape[1]
    n_chunks = d_ff // col_chunk

    def chunk_copies(c):
        cols = pl.ds(c * col_chunk, col_chunk)
        return (pltpu.make_async_copy(wg_hbm.at[layer + (slice(None), cols)], wg_ref.at[:, cols], sem.at[0, c]),
                pltpu.make_async_copy(wu_hbm.at[layer + (slice(None), cols)], wu_ref.at[:, cols], sem.at[1, c]),
                pltpu.make_async_copy(wd_hbm.at[layer + (cols, slice(None))], wd_ref.at[cols, :], sem.at[2, c]))

    def body(wait_for_weights):
        x = x_ref[...]
        h = _rmsnorm(x, nw_ref[...]).astype(BF16)
        for c in range(n_chunks):
            cols = slice(c * col_chunk, (c + 1) * col_chunk)
            if wait_for_weights:
                gate_copy, up_copy, _ = chunk_copies(c)
                gate_copy.wait()
                up_copy.wait()
            g = _dot(h, wg_ref[:, cols].astype(BF16))
            u = _dot(h, wu_ref[:, cols].astype(BF16))
            act_ref[:, cols] = (_silu(g) * u).astype(BF16)
        if wait_for_weights:
            for c in range(n_chunks):
                chunk_copies(c)[2].wait()
        out = x + 0.5 * _dot(act_ref[...], wd_ref[...].astype(BF16))
        if final_norm:
            out = _rmsnorm(out, fnw_ref[...])
        o_ref[...] = out

    @pl.when(pl.program_id(0) == 0)
    def _():
        for c in range(n_chunks):
            gate_copy, up_copy, _ = chunk_copies(c)
            gate_copy.start()
            up_copy.start()
        for c in range(n_chunks):
            chunk_copies(c)[2].start()
        body(True)

    @pl.when(pl.program_id(0) > 0)
    def _():
        body(False)


def _ffn(x, nw, wg, wu, wd, fnw, layer, *, final_norm):
    s, d = x.shape
    d_ff = wg.shape[-1]
    tm = FFN_ROW_TILE
    assert s % tm == 0 and d_ff % FFN_COL_CHUNK == 0
    resident = (4 * _nbytes((tm, d), F32) + 3 * _nbytes((d, d_ff), F32) + _nbytes((tm, d_ff), BF16))
    in_hbm = pl.BlockSpec(memory_space=pl.ANY)
    return pl.pallas_call(
        functools.partial(_ffn_kernel, layer=tuple(layer), col_chunk=FFN_COL_CHUNK, final_norm=final_norm),
        out_shape=jax.ShapeDtypeStruct((s, d), F32),
        grid=(s // tm,),
        in_specs=[_row_tile(tm, d), _resident((1, d)), in_hbm, in_hbm, in_hbm, _resident((1, d))],
        out_specs=_row_tile(tm, d),
        scratch_shapes=[pltpu.VMEM((d, d_ff), F32), pltpu.VMEM((d, d_ff), F32), pltpu.VMEM((d_ff, d), F32),
                        pltpu.VMEM((tm, d_ff), BF16), pltpu.SemaphoreType.DMA((3, d_ff // FFN_COL_CHUNK))],
        compiler_params=pltpu.CompilerParams(dimension_semantics=("arbitrary",),
                                             vmem_limit_bytes=_vmem_limit(resident)),
        name="ffn",
    )(x, nw, wg, wu, wd, fnw)


def _sc_kernel(x_ref, nw_ref, win_ref, cw_ref, wout_ref, o_ref, cu_ref):
    tm, d = x_ref.shape
    k_w = cw_ref.shape[0]
    head = V7X_SUBLANES

    @pl.when(pl.program_id(0) == 0)
    def _():
        cu_ref[0:head, :] = jnp.zeros((head, d), F32)

    x = x_ref[...]
    h = _rmsnorm(x, nw_ref[...]).astype(BF16)
    cg = _dot(h, win_ref[:, d:2 * d].astype(BF16))
    u = _dot(h, win_ref[:, 2 * d:3 * d].astype(BF16))
    cu_ref[head:head + tm, :] = cg * u
    v = _causal_conv(cu_ref[...], cw_ref, slice(0, d), k_w)[head:head + tm]
    cu_ref[0:head, :] = cu_ref[tm:tm + head, :]
    bg = _dot(h, win_ref[:, 0:d].astype(BF16))
    o_ref[...] = x + _dot((bg * v).astype(BF16), wout_ref[...].astype(BF16))


def _shortconv(x, nw, win, cw, wout, layer):
    s, d = x.shape
    tm = SC_ROW_TILE
    assert s % tm == 0
    resident = (4 * _nbytes((tm, d), F32) + _nbytes((d, 3 * d), F32) + _nbytes((d, d), F32)
                + _nbytes((tm + V7X_SUBLANES, d), F32))
    return pl.pallas_call(
        _sc_kernel,
        out_shape=jax.ShapeDtypeStruct((s, d), F32),
        grid=(s // tm,),
        in_specs=[_row_tile(tm, d), _resident((1, d)), _resident_layer(win, layer), _resident_layer(cw, layer),
                  _resident_layer(wout, layer)],
        out_specs=_row_tile(tm, d),
        scratch_shapes=[pltpu.VMEM((tm + V7X_SUBLANES, d), F32)],
        compiler_params=pltpu.CompilerParams(dimension_semantics=("arbitrary",),
                                             vmem_limit_bytes=_vmem_limit(resident)),
        name="shortconv",
    )(x, nw, win, cw, wout)


def _ssd_chunk_decays(rows, dt_ref, alog_ref, tri_ref, expand_ref):
    chunk = SSD_CHUNK
    dt = dt_ref[rows, :]
    ac = dt * (-jnp.exp(alog_ref[...]))
    a_cs = _dot(tri_ref[...], jnp.concatenate(_bf16_terms(ac, SPLIT_TERMS), axis=0))
    a2 = a_cs * LOG2_E
    s_side_t = a2.T - jnp.log2(dt.T)
    tot = a_cs[chunk - 1:chunk, :]
    per_head = jnp.concatenate(
        [jnp.exp(tot - a_cs) * dt, jnp.exp(a_cs), jnp.broadcast_to(jnp.exp(tot), (V7X_SUBLANES, tot.shape[1]))],
        axis=0)
    per_col = _dot(jnp.concatenate(_bf16_terms(per_head, 2), axis=1), expand_ref[...])
    return a2, s_side_t, per_col[0:chunk], per_col[chunk:2 * chunk], per_col[2 * chunk:2 * chunk + 1]


def _ssd_chunk_group(g, rows, decays, xbc_ref, y_ref, state_ref, dskip_ref, *, d_inner):
    a2, s_side_t, w_state, w_off, w_tot = decays
    chunk = SSD_CHUNK
    n_state = state_ref.shape[0]
    n_groups = (xbc_ref.shape[1] - d_inner) // (2 * n_state)
    group_w = d_inner // n_groups
    pair_w = 2 * SSD_HEAD_DIM
    causal = (lax.broadcasted_iota(jnp.int32, (chunk, chunk), 0)
              >= lax.broadcasted_iota(jnp.int32, (chunk, chunk), 1))
    first_head = lax.broadcasted_iota(jnp.int32, (chunk, pair_w), 1) < SSD_HEAD_DIM

    gcols = slice(g * group_w, (g + 1) * group_w)
    xg = xbc_ref[rows, gcols]
    bg = xbc_ref[rows, d_inner + g * n_state:d_inner + (g + 1) * n_state].astype(BF16)
    cg = xbc_ref[rows, d_inner + (n_groups + g) * n_state:d_inner + (n_groups + g + 1) * n_state].astype(BF16)
    cb = lax.dot_general(cg, bg, (((1,), (1,)), ((), ())), preferred_element_type=F32)
    s_old = state_ref[:, gcols]
    y_off = _dot(cg, s_old.astype(BF16)) * w_off[:, gcols]
    wx = (w_state[:, gcols] * xg).astype(BF16)
    state_ref[:, gcols] = w_tot[:, gcols] * s_old + lax.dot_general(
        bg, wx, (((0,), (0,)), ((), ())), preferred_element_type=F32)
    for j in range(group_w // pair_w):
        h0 = (g * group_w + j * pair_w) // SSD_HEAD_DIM
        pcols = slice(g * group_w + j * pair_w, g * group_w + (j + 1) * pair_w)
        m = []
        for hh in (h0, h0 + 1):
            seg2 = a2[:, hh:hh + 1] - s_side_t[hh:hh + 1, :]
            m.append(cb * jnp.exp2(jnp.where(causal, seg2, -jnp.inf)))
        lhs = jnp.concatenate(m, axis=1).astype(BF16)
        xp = xg[:, j * pair_w:(j + 1) * pair_w]
        rhs = jnp.concatenate([jnp.where(first_head, xp, 0.0), jnp.where(first_head, 0.0, xp)],
                              axis=0).astype(BF16)
        y_ref[rows, pcols] = (_dot(lhs, rhs) + y_off[:, j * pair_w:(j + 1) * pair_w]
                              + xp * dskip_ref[:, pcols])


def _ssd_kernel(x_ref, nw_ref, win_ref, wdt_ref, cw_ref, cb_ref, dtb_ref, alog_ref, dskip_ref, gnw_ref,
                wout_ref, tri_ref, expand_ref,
                o_ref,
                h_ref, conv_ref, xbc_ref, dt_ref, y_ref, state_ref, *, col_chunk):
    tm, d = x_ref.shape
    d_inner = wout_ref.shape[0]
    conv_dim = cw_ref.shape[1]
    k_w = cw_ref.shape[0]
    n_groups = (conv_dim - d_inner) // (2 * state_ref.shape[0])
    group_w = d_inner // n_groups
    head = V7X_SUBLANES

    @pl.when(pl.program_id(0) == 0)
    def _():
        conv_ref[0:head, :] = jnp.zeros((head, conv_dim), F32)
        state_ref[...] = jnp.zeros(state_ref.shape, F32)

    x = x_ref[...]
    h = _rmsnorm(x, nw_ref[...]).astype(BF16)
    h_ref[...] = h
    for c in range(conv_dim // col_chunk):
        cols = slice(c * col_chunk, (c + 1) * col_chunk)
        conv_ref[head:head + tm, cols] = _dot(
            h, win_ref[:, d_inner + c * col_chunk:d_inner + (c + 1) * col_chunk])
        acc = _causal_conv(conv_ref[:, cols], cw_ref, cols, k_w)[head:head + tm] + cb_ref[:, cols]
        xbc_ref[:, cols] = _silu(acc)
    conv_ref[0:head, :] = conv_ref[tm:tm + head, :]
    dt_ref[...] = _softplus(_dot(h, wdt_ref[...]) + dtb_ref[...])

    for c in range(tm // SSD_CHUNK):
        rows = slice(c * SSD_CHUNK, (c + 1) * SSD_CHUNK)
        decays = _ssd_chunk_decays(rows, dt_ref, alog_ref, tri_ref, expand_ref)
        for g in range(n_groups):
            _ssd_chunk_group(g, rows, decays, xbc_ref, y_ref, state_ref, dskip_ref, d_inner=d_inner)

    gated = y_ref[...] * _silu(_dot(h_ref[...], win_ref[:, 0:d_inner]))
    parts = []
    for g in range(n_groups):
        gg = gated[:, g * group_w:(g + 1) * group_w]
        parts.append(gg * lax.rsqrt(jnp.mean(gg * gg, axis=-1, keepdims=True) + RMS_EPS))
    normed = (jnp.concatenate(parts, axis=1) * gnw_ref[...]).astype(BF16)
    o_ref[...] = x + _dot(normed, wout_ref[...].astype(BF16))


def _ssd(x, nw, w_in, conv_w, conv_b, dt_bias, a_log, d_skip, norm_w, w_out, layer):
    s, d = x.shape
    d_inner = w_out.shape[-2]
    conv_dim = conv_w.shape[-1]
    n_heads = dt_bias.shape[0]
    hp = V7X_LANES
    assert n_heads <= hp and n_heads * SSD_HEAD_DIM == d_inner
    tm = SSD_ROW_TILE
    assert s % tm == 0 and tm % SSD_CHUNK == 0 and conv_dim % SSD_COL_CHUNK == 0

    def pad_heads(v):
        return jnp.pad(v, [(0, 0)] * (v.ndim - 1) + [(0, hp - n_heads)])

    wdt = pad_heads(w_in[layer][:, d_inner + conv_dim:])
    dtb = pad_heads(dt_bias[None, :])
    alog = pad_heads(a_log[None, :])
    dskip = jnp.repeat(d_skip, SSD_HEAD_DIM)[None, :]
    tri = jnp.tile(jnp.tril(jnp.ones((SSD_CHUNK, SSD_CHUNK), BF16)), (1, SPLIT_TERMS))
    expand = jnp.tile((jnp.arange(hp)[:, None] == (jnp.arange(d_inner) // SSD_HEAD_DIM)[None, :]).astype(BF16),
                      (2, 1))

    scratch = [
        pltpu.VMEM((tm, d), BF16),
        pltpu.VMEM((tm + V7X_SUBLANES, conv_dim), F32),
        pltpu.VMEM((tm, conv_dim), F32),
        pltpu.VMEM((tm, hp), F32),
        pltpu.VMEM((tm, d_inner), F32),
        pltpu.VMEM((SSD_D_STATE, d_inner), F32),
    ]
    resident = (4 * _nbytes((tm, d), F32) + _nbytes(w_in.shape[1:], BF16) + _nbytes(w_out.shape[1:], F32)
                + _nbytes((d, hp), BF16) + _nbytes((2 * hp, d_inner), BF16) + _nbytes((tm, d), BF16)
                + _nbytes((2 * tm + V7X_SUBLANES, conv_dim), F32) + _nbytes((tm, hp), F32)
                + _nbytes((tm + SSD_D_STATE, d_inner), F32))
    return pl.pallas_call(
        functools.partial(_ssd_kernel, col_chunk=SSD_COL_CHUNK),
        out_shape=jax.ShapeDtypeStruct((s, d), F32),
        grid=(s // tm,),
        in_specs=[_row_tile(tm, d), _resident((1, d)), _resident_layer(w_in, layer),
                  _resident((d, hp)), _resident_layer(conv_w, layer), _resident((1, conv_dim)),
                  _resident((1, hp)), _resident((1, hp)), _resident((1, d_inner)), _resident((1, d_inner)),
                  _resident_layer(w_out, layer), _resident(tri.shape), _resident(expand.shape)],
        out_specs=_row_tile(tm, d),
        scratch_shapes=scratch,
        compiler_params=pltpu.CompilerParams(dimension_semantics=("arbitrary",),
                                             vmem_limit_bytes=_vmem_limit(resident)),
        name="ssd",
    )(x, nw, w_in, wdt, conv_w, conv_b[None, :], dtb, alog, dskip, norm_w[None, :], w_out, tri, expand)


def kernel(x, norm_w, ffn_w_gate, ffn_w_up, ffn_w_down, ssd_w_in, ssd_conv_w, ssd_conv_b, ssd_dt_bias,
           ssd_a_log, ssd_d, ssd_norm_w, ssd_w_out, sc_w_in, sc_conv_w, sc_w_out, final_norm_w):
    b, s, d = x.shape
    depth = norm_w.shape[0]
    xs = x.reshape(b * s, d)
    assert b == 1, "causal state is carried across row tiles of one sequence"
    fnw = final_norm_w[None, :]
    ssd_w_in_bf16 = ssd_w_in.astype(BF16)
    for i in range(depth):
        j = i // N_MIXERS
        xs = _ffn(xs, norm_w[i, 0][None, :], ffn_w_gate, ffn_w_up, ffn_w_down, fnw, (i, 0), final_norm=False)
        if i % N_MIXERS == 0:
            xs = _ssd(xs, norm_w[i, 1][None, :], ssd_w_in_bf16, ssd_conv_w, ssd_conv_b[j], ssd_dt_bias[j],
                      ssd_a_log[j], ssd_d[j], ssd_norm_w[j], ssd_w_out, (j,))
        else:
            xs = _shortconv(xs, norm_w[i, 1][None, :], sc_w_in, sc_conv_w, sc_w_out, (j,))
        xs = _ffn(xs, norm_w[i, 2][None, :], ffn_w_gate, ffn_w_up, ffn_w_down, fnw, (i, 1),
                  final_norm=(i == depth - 1))
    return xs.reshape(b, s, d)
```

```python
import functools
import math

import jax
import jax.numpy as jnp
from jax import lax
from jax.experimental import pallas as pl
from jax.experimental.pallas import tpu as pltpu

F32 = jnp.float32
BF16 = jnp.bfloat16

RMS_EPS = 1e-5
SSD_HEAD_DIM = 64
SSD_N_GROUPS = 4
SSD_D_STATE = 128
SSD_CHUNK = 128
N_MIXERS = 2
LOG2_E = 1.4426950408889634

V7X_LANES = 128
V7X_SUBLANES = 8
V7X_VMEM_BYTES = 64 * 1024 * 1024

FFN_ROW_TILE = 1024
FFN_COL_CHUNK = 256
SC_ROW_TILE = 1024
SSD_ROW_TILE = 512
SSD_COL_CHUNK = 512
SPLIT_TERMS = 3


def _vmem_limit(resident_bytes):
    return int(min(2 * resident_bytes, V7X_VMEM_BYTES - 2 * 1024 * 1024))


def _nbytes(shape, dtype):
    return math.prod(shape) * jnp.dtype(dtype).itemsize


def _rmsnorm(x, w):
    inv = lax.rsqrt(jnp.mean(x * x, axis=-1, keepdims=True) + RMS_EPS)
    return (x * inv) * w


def _silu(x):
    return x * jax.nn.sigmoid(x)


def _softplus(x):
    return jnp.maximum(x, 0.0) + jnp.log1p(jnp.exp(-jnp.abs(x)))


def _dot(a, b):
    return jnp.dot(a, b, preferred_element_type=F32)


def _bf16_terms(v, n):
    terms = []
    for _ in range(n - 1):
        t = v.astype(BF16)
        terms.append(t)
        v = v - t.astype(F32)
    terms.append(v.astype(BF16))
    return terms


def _causal_conv(ext, w_ref, cols, k_w):
    t, c = ext.shape
    sub = V7X_SUBLANES
    last_sublane = lax.broadcasted_iota(jnp.int32, (t // sub, sub, c), 1) == sub - 1

    def shift_one_row(a):
        a = a.reshape(t // sub, sub, c)
        return pltpu.roll(jnp.where(last_sublane, pltpu.roll(a, 1, axis=0), a), 1, axis=1).reshape(t, c)

    acc = ext * w_ref[0:1, cols]
    for k in range(1, k_w):
        acc = ext * w_ref[k:k + 1, cols] + shift_one_row(acc)
    return acc


def _resident(shape):
    return pl.BlockSpec(shape, lambda i: (0,) * len(shape), pipeline_mode=pl.Buffered(1))


def _resident_layer(stacked, index):
    lead = len(index)
    shape = (None,) * lead + tuple(stacked.shape[lead:])
    return pl.BlockSpec(shape, lambda i: tuple(index) + (0,) * (stacked.ndim - lead),
                        pipeline_mode=pl.Buffered(1))


def _row_tile(tm, d):
    return pl.BlockSpec((tm, d), lambda i: (i, 0))


def _ffn_kernel(x_ref, nw_ref, wg_hbm, wu_hbm, wd_hbm, fnw_ref, o_ref, wg_ref, wu_ref, wd_ref, act_ref, sem, *,
                layer, col_chunk, final_norm):
    d_ff = wg_ref.shape[1]
    n_chunks = d_ff // col_chunk

    def chunk_copies(c):
        cols = pl.ds(c * col_chunk, col_chunk)
        return (pltpu.make_async_copy(wg_hbm.at[layer + (slice(None), cols)], wg_ref.at[:, cols], sem.at[0, c]),
                pltpu.make_async_copy(wu_hbm.at[layer + (slice(None), cols)], wu_ref.at[:, cols], sem.at[1, c]),
                pltpu.make_async_copy(wd_hbm.at[layer + (cols, slice(None))], wd_ref.at[cols, :], sem.at[2, c]))

    def body(wait_for_weights):
        x = x_ref[...]
        h = _rmsnorm(x, nw_ref[...]).astype(BF16)
        for c in range(n_chunks):
            cols = slice(c * col_chunk, (c + 1) * col_chunk)
            if wait_for_weights:
                gate_copy, up_copy, _ = chunk_copies(c)
                gate_copy.wait()
                up_copy.wait()
            g = _dot(h, wg_ref[:, cols].astype(BF16))
            u = _dot(h, wu_ref[:, cols].astype(BF16))
            act_ref[:, cols] = (_silu(g) * u).astype(BF16)
        if wait_for_weights:
            for c in range(n_chunks):
                chunk_copies(c)[2].wait()
        out = x + 0.5 * _dot(act_ref[...], wd_ref[...].astype(BF16))
        if final_norm:
            out = _rmsnorm(out, fnw_ref[...])
        o_ref[...] = out

    @pl.when(pl.program_id(0) == 0)
    def _():
        for c in range(n_chunks):
            gate_copy, up_copy, _ = chunk_copies(c)
            gate_copy.start()
            up_copy.start()
        for c in range(n_chunks):
            chunk_copies(c)[2].start()
        body(True)

    @pl.when(pl.program_id(0) > 0)
    def _():
        body(False)


def _ffn(x, nw, wg, wu, wd, fnw, layer, *, final_norm):
    s, d = x.shape
    d_ff = wg.shape[-1]
    tm = FFN_ROW_TILE
    assert s % tm == 0 and d_ff % FFN_COL_CHUNK == 0
    resident = (4 * _nbytes((tm, d), F32) + 3 * _nbytes((d, d_ff), F32) + _nbytes((tm, d_ff), BF16))
    in_hbm = pl.BlockSpec(memory_space=pl.ANY)
    return pl.pallas_call(
        functools.partial(_ffn_kernel, layer=tuple(layer), col_chunk=FFN_COL_CHUNK, final_norm=final_norm),
        out_shape=jax.ShapeDtypeStruct((s, d), F32),
        grid=(s // tm,),
        in_specs=[_row_tile(tm, d), _resident((1, d)), in_hbm, in_hbm, in_hbm, _resident((1, d))],
        out_specs=_row_tile(tm, d),
        scratch_shapes=[pltpu.VMEM((d, d_ff), F32), pltpu.VMEM((d, d_ff), F32), pltpu.VMEM((d_ff, d), F32),
                        pltpu.VMEM((tm, d_ff), BF16), pltpu.SemaphoreType.DMA((3, d_ff // FFN_COL_CHUNK))],
        compiler_params=pltpu.CompilerParams(dimension_semantics=("arbitrary",),
                                             vmem_limit_bytes=_vmem_limit(resident)),
        name="ffn",
    )(x, nw, wg, wu, wd, fnw)


def _sc_kernel(x_ref, nw_ref, win_ref, cw_ref, wout_ref, o_ref, cu_ref):
    tm, d = x_ref.shape
    k_w = cw_ref.shape[0]
    head = V7X_SUBLANES

    @pl.when(pl.program_id(0) == 0)
    def _():
        cu_ref[0:head, :] = jnp.zeros((head, d), F32)

    x = x_ref[...]
    h = _rmsnorm(x, nw_ref[...]).astype(BF16)
    cg = _dot(h, win_ref[:, d:2 * d].astype(BF16))
    u = _dot(h, win_ref[:, 2 * d:3 * d].astype(BF16))
    cu_ref[head:head + tm, :] = cg * u
    v = _causal_conv(cu_ref[...], cw_ref, slice(0, d), k_w)[head:head + tm]
    cu_ref[0:head, :] = cu_ref[tm:tm + head, :]
    bg = _dot(h, win_ref[:, 0:d].astype(BF16))
    o_ref[...] = x + _dot((bg * v).astype(BF16), wout_ref[...].astype(BF16))


def _shortconv(x, nw, win, cw, wout, layer):
    s, d = x.shape
    tm = SC_ROW_TILE
    assert s % tm == 0
    resident = (4 * _nbytes((tm, d), F32) + _nbytes((d, 3 * d), F32) + _nbytes((d, d), F32)
                + _nbytes((tm + V7X_SUBLANES, d), F32))
    return pl.pallas_call(
        _sc_kernel,
        out_shape=jax.ShapeDtypeStruct((s, d), F32),
        grid=(s // tm,),
        in_specs=[_row_tile(tm, d), _resident((1, d)), _resident_layer(win, layer), _resident_layer(cw, layer),
                  _resident_layer(wout, layer)],
        out_specs=_row_tile(tm, d),
        scratch_shapes=[pltpu.VMEM((tm + V7X_SUBLANES, d), F32)],
        compiler_params=pltpu.CompilerParams(dimension_semantics=("arbitrary",),
                                             vmem_limit_bytes=_vmem_limit(resident)),
        name="shortconv",
    )(x, nw, win, cw, wout)


def _ssd_chunk_decays(rows, dt_ref, alog_ref, tri_ref, expand_ref):
    chunk = SSD_CHUNK
    dt = dt_ref[rows, :]
    ac = dt * (-jnp.exp(alog_ref[...]))
    a_cs = _dot(tri_ref[...], jnp.concatenate(_bf16_terms(ac, SPLIT_TERMS), axis=0))
    a2 = a_cs * LOG2_E
    s_side_t = a2.T - jnp.log2(dt.T)
    tot = a_cs[chunk - 1:chunk, :]
    per_head = jnp.concatenate(
        [jnp.exp(tot - a_cs) * dt, jnp.exp(a_cs), jnp.broadcast_to(jnp.exp(tot), (V7X_SUBLANES, tot.shape[1]))],
        axis=0)
    per_col = _dot(jnp.concatenate(_bf16_terms(per_head, 2), axis=1), expand_ref[...])
    return a2, s_side_t, per_col[0:chunk], per_col[chunk:2 * chunk], per_col[2 * chunk:2 * chunk + 1]


def _ssd_chunk_group(g, rows, decays, xbc_ref, y_ref, state_ref, dskip_ref, *, d_inner):
    a2, s_side_t, w_state, w_off, w_tot = decays
    chunk = SSD_CHUNK
    n_state = state_ref.shape[0]
    n_groups = (xbc_ref.shape[1] - d_inner) // (2 * n_state)
    group_w = d_inner // n_groups
    pair_w = 2 * SSD_HEAD_DIM
    causal = (lax.broadcasted_iota(jnp.int32, (chunk, chunk), 0)
              >= lax.broadcasted_iota(jnp.int32, (chunk, chunk), 1))
    first_head = lax.broadcasted_iota(jnp.int32, (chunk, pair_w), 1) < SSD_HEAD_DIM

    gcols = slice(g * group_w, (g + 1) * group_w)
    xg = xbc_ref[rows, gcols]
    bg = xbc_ref[rows, d_inner + g * n_state:d_inner + (g + 1) * n_state].astype(BF16)
    cg = xbc_ref[rows, d_inner + (n_groups + g) * n_state:d_inner + (n_groups + g + 1) * n_state].astype(BF16)
    cb = lax.dot_general(cg, bg, (((1,), (1,)), ((), ())), preferred_element_type=F32)
    s_old = state_ref[:, gcols]
    y_off = _dot(cg, s_old.astype(BF16)) * w_off[:, gcols]
    wx = (w_state[:, gcols] * xg).astype(BF16)
    state_ref[:, gcols] = w_tot[:, gcols] * s_old + lax.dot_general(
        bg, wx, (((0,), (0,)), ((), ())), preferred_element_type=F32)
    for j in range(group_w // pair_w):
        h0 = (g * group_w + j * pair_w) // SSD_HEAD_DIM
        pcols = slice(g * group_w + j * pair_w, g * group_w + (j + 1) * pair_w)
        m = []
        for hh in (h0, h0 + 1):
            seg2 = a2[:, hh:hh + 1] - s_side_t[hh:hh + 1, :]
            m.append(cb * jnp.exp2(jnp.where(causal, seg2, -jnp.inf)))
        lhs = jnp.concatenate(m, axis=1).astype(BF16)
        xp = xg[:, j * pair_w:(j + 1) * pair_w]
        rhs = jnp.concatenate([jnp.where(first_head, xp, 0.0), jnp.where(first_head, 0.0, xp)],
                              axis=0).astype(BF16)
        y_ref[rows, pcols] = (_dot(lhs, rhs) + y_off[:, j * pair_w:(j + 1) * pair_w]
                              + xp * dskip_ref[:, pcols])


def _ssd_kernel(x_ref, nw_ref, win_ref, wdt_ref, cw_ref, cb_ref, dtb_ref, alog_ref, dskip_ref, gnw_ref,
                wout_ref, tri_ref, expand_ref,
                o_ref,
                h_ref, conv_ref, xbc_ref, dt_ref, y_ref, state_ref, *, col_chunk):
    tm, d = x_ref.shape
    d_inner = wout_ref.shape[0]
    conv_dim = cw_ref.shape[1]
    k_w = cw_ref.shape[0]
    n_groups = (conv_dim - d_inner) // (2 * state_ref.shape[0])
    group_w = d_inner // n_groups
    head = V7X_SUBLANES

    @pl.when(pl.program_id(0) == 0)
    def _():
        conv_ref[0:head, :] = jnp.zeros((head, conv_dim), F32)
        state_ref[...] = jnp.zeros(state_ref.shape, F32)

    x = x_ref[...]
    h = _rmsnorm(x, nw_ref[...]).astype(BF16)
    h_ref[...] = h
    for c in range(conv_dim // col_chunk):
        cols = slice(c * col_chunk, (c + 1) * col_chunk)
        conv_ref[head:head + tm, cols] = _dot(
            h, win_ref[:, d_inner + c * col_chunk:d_inner + (c + 1) * col_chunk])
        acc = _causal_conv(conv_ref[:, cols], cw_ref, cols, k_w)[head:head + tm] + cb_ref[:, cols]
        xbc_ref[:, cols] = _silu(acc)
    conv_ref[0:head, :] = conv_ref[tm:tm + head, :]
    dt_ref[...] = _softplus(_dot(h, wdt_ref[...]) + dtb_ref[...])

    for c in range(tm // SSD_CHUNK):
        rows = slice(c * SSD_CHUNK, (c + 1) * SSD_CHUNK)
        decays = _ssd_chunk_decays(rows, dt_ref, alog_ref, tri_ref, expand_ref)
        for g in range(n_groups):
            _ssd_chunk_group(g, rows, decays, xbc_ref, y_ref, state_ref, dskip_ref, d_inner=d_inner)

    gated = y_ref[...] * _silu(_dot(h_ref[...], win_ref[:, 0:d_inner]))
    parts = []
    for g in range(n_groups):
        gg = gated[:, g * group_w:(g + 1) * group_w]
        parts.append(gg * lax.rsqrt(jnp.mean(gg * gg, axis=-1, keepdims=True) + RMS_EPS))
    normed = (jnp.concatenate(parts, axis=1) * gnw_ref[...]).astype(BF16)
    o_ref[...] = x + _dot(normed, wout_ref[...].astype(BF16))


def _ssd(x, nw, w_in, conv_w, conv_b, dt_bias, a_log, d_skip, norm_w, w_out, layer):
    s, d = x.shape
    d_inner = w_out.shape[-2]
    conv_dim = conv_w.shape[-1]
    n_heads = dt_bias.shape[0]
    hp = V7X_LANES
    assert n_heads <= hp and n_heads * SSD_HEAD_DIM == d_inner
    tm = SSD_ROW_TILE
    assert s % tm == 0 and tm % SSD_CHUNK == 0 and conv_dim % SSD_COL_CHUNK == 0

    def pad_heads(v):
        return jnp.pad(v, [(0, 0)] * (v.ndim - 1) + [(0, hp - n_heads)])

    wdt = pad_heads(w_in[layer][:, d_inner + conv_dim:])
    dtb = pad_heads(dt_bias[None, :])
    alog = pad_heads(a_log[None, :])
    dskip = jnp.repeat(d_skip, SSD_HEAD_DIM)[None, :]
    tri = jnp.tile(jnp.tril(jnp.ones((SSD_CHUNK, SSD_CHUNK), BF16)), (1, SPLIT_TERMS))
    expand = jnp.tile((jnp.arange(hp)[:, None] == (jnp.arange(d_inner) // SSD_HEAD_DIM)[None, :]).astype(BF16),
                      (2, 1))

    scratch = [
        pltpu.VMEM((tm, d), BF16),
        pltpu.VMEM((tm + V7X_SUBLANES, conv_dim), F32),
        pltpu.VMEM((tm, conv_dim), F32),
        pltpu.VMEM((tm, hp), F32),
        pltpu.VMEM((tm, d_inner), F32),
        pltpu.VMEM((SSD_D_STATE, d_inner), F32),
    ]
    resident = (4 * _nbytes((tm, d), F32) + _nbytes(w_in.shape[1:], BF16) + _nbytes(w_out.shape[1:], F32)
                + _nbytes((d, hp), BF16) + _nbytes((2 * hp, d_inner), BF16) + _nbytes((tm, d), BF16)
                + _nbytes((2 * tm + V7X_SUBLANES, conv_dim), F32) + _nbytes((tm, hp), F32)
                + _nbytes((tm + SSD_D_STATE, d_inner), F32))
    return pl.pallas_call(
        functools.partial(_ssd_kernel, col_chunk=SSD_COL_CHUNK),
        out_shape=jax.ShapeDtypeStruct((s, d), F32),
        grid=(s // tm,),
        in_specs=[_row_tile(tm, d), _resident((1, d)), _resident_layer(w_in, layer),
                  _resident((d, hp)), _resident_layer(conv_w, layer), _resident((1, conv_dim)),
                  _resident((1, hp)), _resident((1, hp)), _resident((1, d_inner)), _resident((1, d_inner)),
                  _resident_layer(w_out, layer), _resident(tri.shape), _resident(expand.shape)],
        out_specs=_row_tile(tm, d),
        scratch_shapes=scratch,
        compiler_params=pltpu.CompilerParams(dimension_semantics=("arbitrary",),
                                             vmem_limit_bytes=_vmem_limit(resident)),
        name="ssd",
    )(x, nw, w_in, wdt, conv_w, conv_b[None, :], dtb, alog, dskip, norm_w[None, :], w_out, tri, expand)


def kernel(x, norm_w, ffn_w_gate, ffn_w_up, ffn_w_down, ssd_w_in, ssd_conv_w, ssd_conv_b, ssd_dt_bias,
           ssd_a_log, ssd_d, ssd_norm_w, ssd_w_out, sc_w_in, sc_conv_w, sc_w_out, final_norm_w):
    b, s, d = x.shape
    depth = norm_w.shape[0]
    xs = x.reshape(b * s, d)
    assert b == 1, "causal state is carried across row tiles of one sequence"
    fnw = final_norm_w[None, :]
    ssd_w_in_bf16 = ssd_w_in.astype(BF16)
    for i in range(depth):
        j = i // N_MIXERS
        xs = _ffn(xs, norm_w[i, 0][None, :], ffn_w_gate, ffn_w_up, ffn_w_down, fnw, (i, 0), final_norm=False)
        if i % N_MIXERS == 0:
            xs = _ssd(xs, norm_w[i, 1][None, :], ssd_w_in_bf16, ssd_conv_w, ssd_conv_b[j], ssd_dt_bias[j],
                      ssd_a_log[j], ssd_d[j], ssd_norm_w[j], ssd_w_out, (j,))
        else:
            xs = _shortconv(xs, norm_w[i, 1][None, :], sc_w_in, sc_conv_w, sc_w_out, (j,))
        xs = _ffn(xs, norm_w[i, 2][None, :], ffn_w_gate, ffn_w_up, ffn_w_down, fnw, (i, 1),
                  final_norm=(i == depth - 1))
    return xs.reshape(b, s, d)
```

```python
import functools
import math

import jax
import jax.numpy as jnp
from jax import lax
from jax.experimental import pallas as pl
from jax.experimental.pallas import tpu as pltpu

F32 = jnp.float32
BF16 = jnp.bfloat16

RMS_EPS = 1e-5
SSD_HEAD_DIM = 64
SSD_N_GROUPS = 4
SSD_D_STATE = 128
SSD_CHUNK = 128
N_MIXERS = 2
LOG2_E = 1.4426950408889634

V7X_LANES = 128
V7X_SUBLANES = 8
V7X_VMEM_BYTES = 64 * 1024 * 1024

FFN_ROW_TILE = 512
FFN_COL_CHUNK = 256
SC_ROW_TILE = 1024
SSD_ROW_TILE = 512
SSD_COL_CHUNK = 512
SPLIT_TERMS = 3


def _vmem_limit(resident_bytes):
    return int(min(2 * resident_bytes, V7X_VMEM_BYTES - 8 * 1024 * 1024))


def _nbytes(shape, dtype):
    return math.prod(shape) * jnp.dtype(dtype).itemsize


def _rmsnorm(x, w):
    inv = lax.rsqrt(jnp.mean(x * x, axis=-1, keepdims=True) + RMS_EPS)
    return (x * inv) * w


def _silu(x):
    return x * jax.nn.sigmoid(x)


def _softplus(x):
    return jnp.maximum(x, 0.0) + jnp.log1p(jnp.exp(-jnp.abs(x)))


def _dot(a, b):
    return jnp.dot(a, b, preferred_element_type=F32)


def _bf16_terms(v, n):
    terms = []
    for _ in range(n - 1):
        t = v.astype(BF16)
        terms.append(t)
        v = v - t.astype(F32)
    terms.append(v.astype(BF16))
    return terms


def _causal_conv(ext, w_ref, cols, k_w):
    t, c = ext.shape
    sub = V7X_SUBLANES
    last_sublane = lax.broadcasted_iota(jnp.int32, (t // sub, sub, c), 1) == sub - 1

    def shift_one_row(a):
        a = a.reshape(t // sub, sub, c)
        return pltpu.roll(jnp.where(last_sublane, pltpu.roll(a, 1, axis=0), a), 1, axis=1).reshape(t, c)

    acc = ext * w_ref[0:1, cols]
    for k in range(1, k_w):
        acc = ext * w_ref[k:k + 1, cols] + shift_one_row(acc)
    return acc


def _resident(shape):
    return pl.BlockSpec(shape, lambda i: (0,) * len(shape), pipeline_mode=pl.Buffered(1))


def _resident_layer(stacked, index):
    lead = len(index)
    shape = (None,) * lead + tuple(stacked.shape[lead:])
    return pl.BlockSpec(shape, lambda i: tuple(index) + (0,) * (stacked.ndim - lead),
                        pipeline_mode=pl.Buffered(1))


def _row_tile(tm, d):
    return pl.BlockSpec((tm, d), lambda i: (i, 0))


def _ffn_kernel(x_ref, nw_ref, wg_hbm, wu_hbm, wd_hbm, fnw_ref, o_ref, wg_ref, wu_ref, wd_ref, act_ref, sem, *,
                layer, col_chunk, final_norm):
    d_ff = wg_ref.shape[1]
    n_chunks = d_ff // col_chunk

    def chunk_copies(c):
        cols = pl.ds(c * col_chunk, col_chunk)
        return (pltpu.make_async_copy(wg_hbm.at[layer + (slice(None), cols)], wg_ref.at[:, cols], sem.at[0, c]),
                pltpu.make_async_copy(wu_hbm.at[layer + (slice(None), cols)], wu_ref.at[:, cols], sem.at[1, c]),
                pltpu.make_async_copy(wd_hbm.at[layer + (cols, slice(None))], wd_ref.at[cols, :], sem.at[2, c]))

    def body(wait_for_weights):
        x = x_ref[...]
        xw = (x * nw_ref[...]).astype(BF16)
        inv = lax.rsqrt(jnp.mean(x * x, axis=-1, keepdims=True) + RMS_EPS)
        for c in range(n_chunks):
            cols = slice(c * col_chunk, (c + 1) * col_chunk)
            if wait_for_weights:
                gate_copy, up_copy, _ = chunk_copies(c)
                gate_copy.wait()
                up_copy.wait()
            g = _dot(xw, wg_ref[:, cols].astype(BF16)) * inv
            u = _dot(xw, wu_ref[:, cols].astype(BF16)) * inv
            act_ref[:, cols] = (_silu(g) * u).astype(BF16)
        if wait_for_weights:
            for c in range(n_chunks):
                chunk_copies(c)[2].wait()
        out = x + 0.5 * _dot(act_ref[...], wd_ref[...].astype(BF16))
        if final_norm:
            out = _rmsnorm(out, fnw_ref[...])
        o_ref[...] = out

    @pl.when(pl.program_id(0) == 0)
    def _():
        for c in range(n_chunks):
            gate_copy, up_copy, _ = chunk_copies(c)
            gate_copy.start()
            up_copy.start()
        for c in range(n_chunks):
            chunk_copies(c)[2].start()
        body(True)

    @pl.when(pl.program_id(0) > 0)
    def _():
        body(False)


def _ffn(x, nw, wg, wu, wd, fnw, layer, *, final_norm):
    s, d = x.shape
    d_ff = wg.shape[-1]
    tm = FFN_ROW_TILE
    assert s % tm == 0 and d_ff % FFN_COL_CHUNK == 0
    resident = (4 * _nbytes((tm, d), F32) + 3 * _nbytes((d, d_ff), F32) + _nbytes((tm, d_ff), BF16))
    in_hbm = pl.BlockSpec(memory_space=pl.ANY)
    return pl.pallas_call(
        functools.partial(_ffn_kernel, layer=tuple(layer), col_chunk=FFN_COL_CHUNK, final_norm=final_norm),
        out_shape=jax.ShapeDtypeStruct((s, d), F32),
        grid=(s // tm,),
        in_specs=[_row_tile(tm, d), _resident((1, d)), in_hbm, in_hbm, in_hbm, _resident((1, d))],
        out_specs=_row_tile(tm, d),
        scratch_shapes=[pltpu.VMEM((d, d_ff), F32), pltpu.VMEM((d, d_ff), F32), pltpu.VMEM((d_ff, d), F32),
                        pltpu.VMEM((tm, d_ff), BF16), pltpu.SemaphoreType.DMA((3, d_ff // FFN_COL_CHUNK))],
        compiler_params=pltpu.CompilerParams(dimension_semantics=("arbitrary",),
                                             vmem_limit_bytes=_vmem_limit(resident)),
        name="ffn",
    )(x, nw, wg, wu, wd, fnw)


def _sc_kernel(x_ref, nw_ref, win_ref, cw_ref, wout_ref, o_ref, cu_ref):
    tm, d = x_ref.shape
    k_w = cw_ref.shape[0]
    head = V7X_SUBLANES

    @pl.when(pl.program_id(0) == 0)
    def _():
        cu_ref[0:head, :] = jnp.zeros((head, d), F32)

    x = x_ref[...]
    xw = (x * nw_ref[...]).astype(BF16)
    inv = lax.rsqrt(jnp.mean(x * x, axis=-1, keepdims=True) + RMS_EPS)
    cg = _dot(xw, win_ref[:, d:2 * d].astype(BF16))
    u = _dot(xw, win_ref[:, 2 * d:3 * d].astype(BF16))
    cu_ref[head:head + tm, :] = (cg * u) * (inv * inv)
    v = _causal_conv(cu_ref[...], cw_ref, slice(0, d), k_w)[head:head + tm]
    cu_ref[0:head, :] = cu_ref[tm:tm + head, :]
    bg = _dot(xw, win_ref[:, 0:d].astype(BF16)) * inv
    o_ref[...] = x + _dot((bg * v).astype(BF16), wout_ref[...].astype(BF16))


def _shortconv(x, nw, win, cw, wout, layer):
    s, d = x.shape
    tm = SC_ROW_TILE
    assert s % tm == 0
    resident = (4 * _nbytes((tm, d), F32) + _nbytes((d, 3 * d), F32) + _nbytes((d, d), F32)
                + _nbytes((tm + V7X_SUBLANES, d), F32))
    return pl.pallas_call(
        _sc_kernel,
        out_shape=jax.ShapeDtypeStruct((s, d), F32),
        grid=(s // tm,),
        in_specs=[_row_tile(tm, d), _resident((1, d)), _resident_layer(win, layer), _resident_layer(cw, layer),
                  _resident_layer(wout, layer)],
        out_specs=_row_tile(tm, d),
        scratch_shapes=[pltpu.VMEM((tm + V7X_SUBLANES, d), F32)],
        compiler_params=pltpu.CompilerParams(dimension_semantics=("arbitrary",),
                                             vmem_limit_bytes=_vmem_limit(resident)),
        name="shortconv",
    )(x, nw, win, cw, wout)


def _ssd_chunk_decays(rows, dt_ref, alog_ref, tri_ref, expand_ref):
    chunk = SSD_CHUNK
    dt = dt_ref[rows, :]
    ac = dt * (-jnp.exp(alog_ref[...]))
    a_cs = _dot(tri_ref[...], jnp.concatenate(_bf16_terms(ac, SPLIT_TERMS), axis=0))
    a2 = a_cs * LOG2_E
    s_side_t = a2.T - jnp.log2(dt.T)
    tot = a_cs[chunk - 1:chunk, :]
    per_head = jnp.concatenate(
        [jnp.exp(tot - a_cs) * dt, jnp.exp(a_cs), jnp.broadcast_to(jnp.exp(tot), (V7X_SUBLANES, tot.shape[1]))],
        axis=0)
    per_col = _dot(jnp.concatenate(_bf16_terms(per_head, 2), axis=1), expand_ref[...])
    return a2, s_side_t, per_col[0:chunk], per_col[chunk:2 * chunk], per_col[2 * chunk:2 * chunk + 1]


def _ssd_chunk_group(g, rows, decays, xbc_ref, y_ref, state_ref, dskip_ref, *, d_inner):
    a2, s_side_t, w_state, w_off, w_tot = decays
    chunk = SSD_CHUNK
    n_state = state_ref.shape[0]
    n_groups = (xbc_ref.shape[1] - d_inner) // (2 * n_state)
    group_w = d_inner // n_groups
    pair_w = 2 * SSD_HEAD_DIM
    causal = (lax.broadcasted_iota(jnp.int32, (chunk, chunk), 0)
              >= lax.broadcasted_iota(jnp.int32, (chunk, chunk), 1))
    first_head = lax.broadcasted_iota(jnp.int32, (chunk, pair_w), 1) < SSD_HEAD_DIM

    gcols = slice(g * group_w, (g + 1) * group_w)
    xg = xbc_ref[rows, gcols]
    bg = xbc_ref[rows, d_inner + g * n_state:d_inner + (g + 1) * n_state].astype(BF16)
    cg = xbc_ref[rows, d_inner + (n_groups + g) * n_state:d_inner + (n_groups + g + 1) * n_state].astype(BF16)
    cb = lax.dot_general(cg, bg, (((1,), (1,)), ((), ())), preferred_element_type=F32)
    s_old = state_ref[:, gcols]
    y_off = _dot(cg, s_old.astype(BF16)) * w_off[:, gcols]
    wx = (w_state[:, gcols] * xg).astype(BF16)
    state_ref[:, gcols] = w_tot[:, gcols] * s_old + lax.dot_general(
        bg, wx, (((0,), (0,)), ((), ())), preferred_element_type=F32)
    for j in range(group_w // pair_w):
        h0 = (g * group_w + j * pair_w) // SSD_HEAD_DIM
        pcols = slice(g * group_w + j * pair_w, g * group_w + (j + 1) * pair_w)
        m = []
        for hh in (h0, h0 + 1):
            seg2 = a2[:, hh:hh + 1] - s_side_t[hh:hh + 1, :]
            m.append(cb * jnp.exp2(jnp.where(causal, seg2, -jnp.inf)))
        lhs = jnp.concatenate(m, axis=1).astype(BF16)
        xp = xg[:, j * pair_w:(j + 1) * pair_w]
        rhs = jnp.concatenate([jnp.where(first_head, xp, 0.0), jnp.where(first_head, 0.0, xp)],
                              axis=0).astype(BF16)
        y_ref[rows, pcols] = (_dot(lhs, rhs) + y_off[:, j * pair_w:(j + 1) * pair_w]
                              + xp * dskip_ref[:, pcols])


def _ssd_kernel(x_ref, nw_ref, win_ref, wdt_ref, cw_ref, cb_ref, dtb_ref, alog_ref, dskip_ref, gnw_ref,
                wout_ref, tri_ref, expand_ref,
                o_ref,
                h_ref, conv_ref, xbc_ref, dt_ref, y_ref, state_ref, *, col_chunk):
    tm, d = x_ref.shape
    d_inner = wout_ref.shape[0]
    conv_dim = cw_ref.shape[1]
    k_w = cw_ref.shape[0]
    n_groups = (conv_dim - d_inner) // (2 * state_ref.shape[0])
    group_w = d_inner // n_groups
    head = V7X_SUBLANES

    @pl.when(pl.program_id(0) == 0)
    def _():
        conv_ref[0:head, :] = jnp.zeros((head, conv_dim), F32)
        state_ref[...] = jnp.zeros(state_ref.shape, F32)

    x = x_ref[...]
    h = _rmsnorm(x, nw_ref[...]).astype(BF16)
    h_ref[...] = h
    for c in range(conv_dim // col_chunk):
        cols = slice(c * col_chunk, (c + 1) * col_chunk)
        conv_ref[head:head + tm, cols] = _dot(
            h, win_ref[:, d_inner + c * col_chunk:d_inner + (c + 1) * col_chunk])
        acc = _causal_conv(conv_ref[:, cols], cw_ref, cols, k_w)[head:head + tm] + cb_ref[:, cols]
        xbc_ref[:, cols] = _silu(acc)
    conv_ref[0:head, :] = conv_ref[tm:tm + head, :]
    dt_ref[...] = _softplus(_dot(h, wdt_ref[...]) + dtb_ref[...])

    for c in range(tm // SSD_CHUNK):
        rows = slice(c * SSD_CHUNK, (c + 1) * SSD_CHUNK)
        decays = _ssd_chunk_decays(rows, dt_ref, alog_ref, tri_ref, expand_ref)
        for g in range(n_groups):
            _ssd_chunk_group(g, rows, decays, xbc_ref, y_ref, state_ref, dskip_ref, d_inner=d_inner)

    gated = y_ref[...] * _silu(_dot(h_ref[...], win_ref[:, 0:d_inner]))
    parts = []
    for g in range(n_groups):
        gg = gated[:, g * group_w:(g + 1) * group_w]
        parts.append(gg * lax.rsqrt(jnp.mean(gg * gg, axis=-1, keepdims=True) + RMS_EPS))
    normed = (jnp.concatenate(parts, axis=1) * gnw_ref[...]).astype(BF16)
    o_ref[...] = x + _dot(normed, wout_ref[...].astype(BF16))


def _ssd(x, nw, w_in, conv_w, conv_b, dt_bias, a_log, d_skip, norm_w, w_out, layer):
    s, d = x.shape
    d_inner = w_out.shape[-2]
    conv_dim = conv_w.shape[-1]
    n_heads = dt_bias.shape[0]
    hp = V7X_LANES
    assert n_heads <= hp and n_heads * SSD_HEAD_DIM == d_inner
    tm = SSD_ROW_TILE
    assert s % tm == 0 and tm % SSD_CHUNK == 0 and conv_dim % SSD_COL_CHUNK == 0

    def pad_heads(v):
        return jnp.pad(v, [(0, 0)] * (v.ndim - 1) + [(0, hp - n_heads)])

    wdt = pad_heads(w_in[layer][:, d_inner + conv_dim:])
    dtb = pad_heads(dt_bias[None, :])
    alog = pad_heads(a_log[None, :])
    dskip = jnp.repeat(d_skip, SSD_HEAD_DIM)[None, :]
    tri = jnp.tile(jnp.tril(jnp.ones((SSD_CHUNK, SSD_CHUNK), BF16)), (1, SPLIT_TERMS))
    expand = jnp.tile((jnp.arange(hp)[:, None] == (jnp.arange(d_inner) // SSD_HEAD_DIM)[None, :]).astype(BF16),
                      (2, 1))

    scratch = [
        pltpu.VMEM((tm, d), BF16),
        pltpu.VMEM((tm + V7X_SUBLANES, conv_dim), F32),
        pltpu.VMEM((tm, conv_dim), F32),
        pltpu.VMEM((tm, hp), F32),
        pltpu.VMEM((tm, d_inner), F32),
        pltpu.VMEM((SSD_D_STATE, d_inner), F32),
    ]
    resident = (4 * _nbytes((tm, d), F32) + _nbytes(w_in.shape[1:], BF16) + _nbytes(w_out.shape[1:], F32)
                + _nbytes((d, hp), BF16) + _nbytes((2 * hp, d_inner), BF16) + _nbytes((tm, d), BF16)
                + _nbytes((2 * tm + V7X_SUBLANES, conv_dim), F32) + _nbytes((tm, hp), F32)
                + _nbytes((tm + SSD_D_STATE, d_inner), F32))
    return pl.pallas_call(
        functools.partial(_ssd_kernel, col_chunk=SSD_COL_CHUNK),
        out_shape=jax.ShapeDtypeStruct((s, d), F32),
        grid=(s // tm,),
        in_specs=[_row_tile(tm, d), _resident((1, d)), _resident_layer(w_in, layer),
                  _resident((d, hp)), _resident_layer(conv_w, layer), _resident((1, conv_dim)),
                  _resident((1, hp)), _resident((1, hp)), _resident((1, d_inner)), _resident((1, d_inner)),
                  _resident_layer(w_out, layer), _resident(tri.shape), _resident(expand.shape)],
        out_specs=_row_tile(tm, d),
        scratch_shapes=scratch,
        compiler_params=pltpu.CompilerParams(dimension_semantics=("arbitrary",),
                                             vmem_limit_bytes=_vmem_limit(resident)),
        name="ssd",
    )(x, nw, w_in, wdt, conv_w, conv_b[None, :], dtb, alog, dskip, norm_w[None, :], w_out, tri, expand)


def kernel(x, norm_w, ffn_w_gate, ffn_w_up, ffn_w_down, ssd_w_in, ssd_conv_w, ssd_conv_b, ssd_dt_bias,
           ssd_a_log, ssd_d, ssd_norm_w, ssd_w_out, sc_w_in, sc_conv_w, sc_w_out, final_norm_w):
    b, s, d = x.shape
    depth = norm_w.shape[0]
    xs = x.reshape(b * s, d)
    assert b == 1, "causal state is carried across row tiles of one sequence"
    fnw = final_norm_w[None, :]
    ssd_w_in_bf16 = ssd_w_in.astype(BF16)
    for i in range(depth):
        j = i // N_MIXERS
        xs = _ffn(xs, norm_w[i, 0][None, :], ffn_w_gate, ffn_w_up, ffn_w_down, fnw, (i, 0), final_norm=False)
        if i % N_MIXERS == 0:
            xs = _ssd(xs, norm_w[i, 1][None, :], ssd_w_in_bf16, ssd_conv_w, ssd_conv_b[j], ssd_dt_bias[j],
                      ssd_a_log[j], ssd_d[j], ssd_norm_w[j], ssd_w_out, (j,))
        else:
            xs = _shortconv(xs, norm_w[i, 1][None, :], sc_w_in, sc_conv_w, sc_w_out, (j,))
        xs = _ffn(xs, norm_w[i, 2][None, :], ffn_w_gate, ffn_w_up, ffn_w_down, fnw, (i, 1),
                  final_norm=(i == depth - 1))
    return xs.reshape(b, s, d)
```

```python
import functools
import math

import jax
import jax.numpy as jnp
from jax import lax
from jax.experimental import pallas as pl
from jax.experimental.pallas import tpu as pltpu

F32 = jnp.float32
BF16 = jnp.bfloat16

RMS_EPS = 1e-5
SSD_HEAD_DIM = 64
SSD_N_GROUPS = 4
SSD_D_STATE = 128
SSD_CHUNK = 128
N_MIXERS = 2
LOG2_E = 1.4426950408889634

V7X_LANES = 128
V7X_SUBLANES = 8
V7X_VMEM_BYTES = 64 * 1024 * 1024

FFN_ROW_TILE = 512
FFN_COL_CHUNK = 256
SC_ROW_TILE = 1024
SSD_ROW_TILE = 512
SSD_COL_CHUNK = 512
SPLIT_TERMS = 3


def _vmem_limit(resident_bytes):
    return int(min(2 * resident_bytes, V7X_VMEM_BYTES - 8 * 1024 * 1024))


def _nbytes(shape, dtype):
    return math.prod(shape) * jnp.dtype(dtype).itemsize


def _rmsnorm(x, w):
    inv = lax.rsqrt(jnp.mean(x * x, axis=-1, keepdims=True) + RMS_EPS)
    return (x * inv) * w


def _silu(x):
    return x * jax.nn.sigmoid(x)


def _softplus(x):
    return jnp.maximum(x, 0.0) + jnp.log1p(jnp.exp(-jnp.abs(x)))


def _dot(a, b):
    return jnp.dot(a, b, preferred_element_type=F32)


def _bf16_terms(v, n):
    terms = []
    for _ in range(n - 1):
        t = v.astype(BF16)
        terms.append(t)
        v = v - t.astype(F32)
    terms.append(v.astype(BF16))
    return terms


def _causal_conv(ext, w_ref, cols, k_w):
    t, c = ext.shape
    sub = V7X_SUBLANES
    last_sublane = lax.broadcasted_iota(jnp.int32, (t // sub, sub, c), 1) == sub - 1

    def shift_one_row(a):
        a = a.reshape(t // sub, sub, c)
        return pltpu.roll(jnp.where(last_sublane, pltpu.roll(a, 1, axis=0), a), 1, axis=1).reshape(t, c)

    acc = ext * w_ref[0:1, cols]
    for k in range(1, k_w):
        acc = ext * w_ref[k:k + 1, cols] + shift_one_row(acc)
    return acc


def _resident(shape):
    return pl.BlockSpec(shape, lambda i: (0,) * len(shape), pipeline_mode=pl.Buffered(1))


def _resident_layer(stacked, index):
    lead = len(index)
    shape = (None,) * lead + tuple(stacked.shape[lead:])
    return pl.BlockSpec(shape, lambda i: tuple(index) + (0,) * (stacked.ndim - lead),
                        pipeline_mode=pl.Buffered(1))


def _row_tile(tm, d):
    return pl.BlockSpec((tm, d), lambda i: (i, 0))


def _ffn_kernel(x_ref, nw_ref, wg_hbm, wu_hbm, wd_hbm, fnw_ref, o_ref, wg_ref, wu_ref, wd_ref, act_ref, sem, *,
                layer, col_chunk, final_norm):
    d_ff = wg_ref.shape[1]
    n_chunks = d_ff // col_chunk

    def chunk_copies(c):
        cols = pl.ds(c * col_chunk, col_chunk)
        return (pltpu.make_async_copy(wg_hbm.at[layer + (slice(None), cols)], wg_ref.at[:, cols], sem.at[0, c]),
                pltpu.make_async_copy(wu_hbm.at[layer + (slice(None), cols)], wu_ref.at[:, cols], sem.at[1, c]),
                pltpu.make_async_copy(wd_hbm.at[layer + (cols, slice(None))], wd_ref.at[cols, :], sem.at[2, c]))

    def body(wait_for_weights):
        x = x_ref[...]
        xw = (x * nw_ref[...]).astype(BF16)
        inv = lax.rsqrt(jnp.mean(x * x, axis=-1, keepdims=True) + RMS_EPS)
        for c in range(n_chunks):
            cols = slice(c * col_chunk, (c + 1) * col_chunk)
            if wait_for_weights:
                gate_copy, up_copy, _ = chunk_copies(c)
                gate_copy.wait()
                up_copy.wait()
            g = _dot(xw, wg_ref[:, cols].astype(BF16)) * inv
            u = _dot(xw, wu_ref[:, cols].astype(BF16)) * inv
            act_ref[:, cols] = (_silu(g) * u).astype(BF16)
        if wait_for_weights:
            for c in range(n_chunks):
                chunk_copies(c)[2].wait()
        out = x + 0.5 * _dot(act_ref[...], wd_ref[...].astype(BF16))
        if final_norm:
            out = _rmsnorm(out, fnw_ref[...])
        o_ref[...] = out

    @pl.when(pl.program_id(0) == 0)
    def _():
        for c in range(n_chunks):
            gate_copy, up_copy, _ = chunk_copies(c)
            gate_copy.start()
            up_copy.start()
        for c in range(n_chunks):
            chunk_copies(c)[2].start()
        body(True)

    @pl.when(pl.program_id(0) > 0)
    def _():
        body(False)


def _ffn(x, nw, wg, wu, wd, fnw, layer, *, final_norm):
    s, d = x.shape
    d_ff = wg.shape[-1]
    tm = FFN_ROW_TILE
    assert s % tm == 0 and d_ff % FFN_COL_CHUNK == 0
    resident = (4 * _nbytes((tm, d), F32) + 3 * _nbytes((d, d_ff), F32) + _nbytes((tm, d_ff), BF16))
    in_hbm = pl.BlockSpec(memory_space=pl.ANY)
    return pl.pallas_call(
        functools.partial(_ffn_kernel, layer=tuple(layer), col_chunk=FFN_COL_CHUNK, final_norm=final_norm),
        out_shape=jax.ShapeDtypeStruct((s, d), F32),
        grid=(s // tm,),
        in_specs=[_row_tile(tm, d), _resident((1, d)), in_hbm, in_hbm, in_hbm, _resident((1, d))],
        out_specs=_row_tile(tm, d),
        scratch_shapes=[pltpu.VMEM((d, d_ff), F32), pltpu.VMEM((d, d_ff), F32), pltpu.VMEM((d_ff, d), F32),
                        pltpu.VMEM((tm, d_ff), BF16), pltpu.SemaphoreType.DMA((3, d_ff // FFN_COL_CHUNK))],
        compiler_params=pltpu.CompilerParams(dimension_semantics=("arbitrary",),
                                             vmem_limit_bytes=_vmem_limit(resident)),
        name="ffn",
    )(x, nw, wg, wu, wd, fnw)


def _sc_kernel(x_ref, nw_ref, win_hbm, cw_ref, wout_hbm, o_ref, win_ref, wout_ref, cu_ref, sem, *, layer):
    tm, d = x_ref.shape
    k_w = cw_ref.shape[0]
    head = V7X_SUBLANES
    c_cols, u_cols, b_cols = slice(d, 2 * d), slice(2 * d, 3 * d), slice(0, d)

    def weight_copies():
        parts = [pltpu.make_async_copy(win_hbm.at[layer + (slice(None), cols)], win_ref.at[:, cols], sem.at[k])
                 for k, cols in enumerate((c_cols, u_cols, b_cols))]
        return parts + [pltpu.make_async_copy(wout_hbm.at[layer], wout_ref, sem.at[len(parts)])]

    def body(wait_for_weights):
        c_copy, u_copy, b_copy, out_copy = weight_copies()
        x = x_ref[...]
        xw = (x * nw_ref[...]).astype(BF16)
        inv = lax.rsqrt(jnp.mean(x * x, axis=-1, keepdims=True) + RMS_EPS)
        if wait_for_weights:
            c_copy.wait()
        cg = _dot(xw, win_ref[:, c_cols].astype(BF16))
        if wait_for_weights:
            u_copy.wait()
        u = _dot(xw, win_ref[:, u_cols].astype(BF16))
        cu_ref[head:head + tm, :] = (cg * u) * (inv * inv)
        v = _causal_conv(cu_ref[...], cw_ref, slice(0, d), k_w)[head:head + tm]
        cu_ref[0:head, :] = cu_ref[tm:tm + head, :]
        if wait_for_weights:
            b_copy.wait()
        bg = _dot(xw, win_ref[:, b_cols].astype(BF16)) * inv
        if wait_for_weights:
            out_copy.wait()
        o_ref[...] = x + _dot((bg * v).astype(BF16), wout_ref[...].astype(BF16))

    @pl.when(pl.program_id(0) == 0)
    def _():
        cu_ref[0:head, :] = jnp.zeros((head, d), F32)
        for copy in weight_copies():
            copy.start()
        body(True)

    @pl.when(pl.program_id(0) > 0)
    def _():
        body(False)


def _shortconv(x, nw, win, cw, wout, layer):
    s, d = x.shape
    tm = SC_ROW_TILE
    assert s % tm == 0
    resident = (4 * _nbytes((tm, d), F32) + _nbytes((d, 3 * d), F32) + _nbytes((d, d), F32)
                + _nbytes((tm + V7X_SUBLANES, d), F32))
    in_hbm = pl.BlockSpec(memory_space=pl.ANY)
    return pl.pallas_call(
        functools.partial(_sc_kernel, layer=tuple(layer)),
        out_shape=jax.ShapeDtypeStruct((s, d), F32),
        grid=(s // tm,),
        in_specs=[_row_tile(tm, d), _resident((1, d)), in_hbm, _resident_layer(cw, layer), in_hbm],
        out_specs=_row_tile(tm, d),
        scratch_shapes=[pltpu.VMEM((d, 3 * d), F32), pltpu.VMEM((d, d), F32),
                        pltpu.VMEM((tm + V7X_SUBLANES, d), F32), pltpu.SemaphoreType.DMA((4,))],
        compiler_params=pltpu.CompilerParams(dimension_semantics=("arbitrary",),
                                             vmem_limit_bytes=_vmem_limit(resident)),
        name="shortconv",
    )(x, nw, win, cw, wout)


def _ssd_chunk_decays(rows, dt_ref, alog_ref, tri_ref, expand_ref):
    chunk = SSD_CHUNK
    dt = dt_ref[rows, :]
    ac = dt * (-jnp.exp(alog_ref[...]))
    a_cs = _dot(tri_ref[...], jnp.concatenate(_bf16_terms(ac, SPLIT_TERMS), axis=0))
    a2 = a_cs * LOG2_E
    s_side_t = a2.T - jnp.log2(dt.T)
    tot = a_cs[chunk - 1:chunk, :]
    per_head = jnp.concatenate(
        [jnp.exp(tot - a_cs) * dt, jnp.exp(a_cs), jnp.broadcast_to(jnp.exp(tot), (V7X_SUBLANES, tot.shape[1]))],
        axis=0)
    per_col = _dot(jnp.concatenate(_bf16_terms(per_head, 2), axis=1), expand_ref[...])
    return a2, s_side_t, per_col[0:chunk], per_col[chunk:2 * chunk], per_col[2 * chunk:2 * chunk + 1]


def _ssd_chunk_group(g, rows, decays, xbc_ref, y_ref, state_ref, dskip_ref, *, d_inner):
    a2, s_side_t, w_state, w_off, w_tot = decays
    chunk = SSD_CHUNK
    n_state = state_ref.shape[0]
    n_groups = (xbc_ref.shape[1] - d_inner) // (2 * n_state)
    group_w = d_inner // n_groups
    pair_w = 2 * SSD_HEAD_DIM
    causal = (lax.broadcasted_iota(jnp.int32, (chunk, chunk), 0)
              >= lax.broadcasted_iota(jnp.int32, (chunk, chunk), 1))
    first_head = lax.broadcasted_iota(jnp.int32, (chunk, pair_w), 1) < SSD_HEAD_DIM

    gcols = slice(g * group_w, (g + 1) * group_w)
    xg = xbc_ref[rows, gcols]
    bg = xbc_ref[rows, d_inner + g * n_state:d_inner + (g + 1) * n_state].astype(BF16)
    cg = xbc_ref[rows, d_inner + (n_groups + g) * n_state:d_inner + (n_groups + g + 1) * n_state].astype(BF16)
    cb = lax.dot_general(cg, bg, (((1,), (1,)), ((), ())), preferred_element_type=F32)
    s_old = state_ref[:, gcols]
    y_off = _dot(cg, s_old.astype(BF16)) * w_off[:, gcols]
    wx = (w_state[:, gcols] * xg).astype(BF16)
    state_ref[:, gcols] = w_tot[:, gcols] * s_old + lax.dot_general(
        bg, wx, (((0,), (0,)), ((), ())), preferred_element_type=F32)
    for j in range(group_w // pair_w):
        h0 = (g * group_w + j * pair_w) // SSD_HEAD_DIM
        pcols = slice(g * group_w + j * pair_w, g * group_w + (j + 1) * pair_w)
        m = []
        for hh in (h0, h0 + 1):
            seg2 = a2[:, hh:hh + 1] - s_side_t[hh:hh + 1, :]
            m.append(cb * jnp.exp2(jnp.where(causal, seg2, -jnp.inf)))
        lhs = jnp.concatenate(m, axis=1).astype(BF16)
        xp = xg[:, j * pair_w:(j + 1) * pair_w]
        rhs = jnp.concatenate([jnp.where(first_head, xp, 0.0), jnp.where(first_head, 0.0, xp)],
                              axis=0).astype(BF16)
        y_ref[rows, pcols] = (_dot(lhs, rhs) + y_off[:, j * pair_w:(j + 1) * pair_w]
                              + xp * dskip_ref[:, pcols])


def _ssd_kernel(x_ref, nw_ref, win_ref, wdt_ref, cw_ref, cb_ref, dtb_ref, alog_ref, dskip_ref, gnw_ref,
                wout_ref, tri_ref, expand_ref,
                o_ref,
                h_ref, conv_ref, xbc_ref, dt_ref, y_ref, state_ref, *, col_chunk):
    tm, d = x_ref.shape
    d_inner = wout_ref.shape[0]
    conv_dim = cw_ref.shape[1]
    k_w = cw_ref.shape[0]
    n_groups = (conv_dim - d_inner) // (2 * state_ref.shape[0])
    group_w = d_inner // n_groups
    head = V7X_SUBLANES

    @pl.when(pl.program_id(0) == 0)
    def _():
        conv_ref[0:head, :] = jnp.zeros((head, conv_dim), F32)
        state_ref[...] = jnp.zeros(state_ref.shape, F32)

    x = x_ref[...]
    h = _rmsnorm(x, nw_ref[...]).astype(BF16)
    h_ref[...] = h
    for c in range(conv_dim // col_chunk):
        cols = slice(c * col_chunk, (c + 1) * col_chunk)
        conv_ref[head:head + tm, cols] = _dot(
            h, win_ref[:, d_inner + c * col_chunk:d_inner + (c + 1) * col_chunk])
        acc = _causal_conv(conv_ref[:, cols], cw_ref, cols, k_w)[head:head + tm] + cb_ref[:, cols]
        xbc_ref[:, cols] = _silu(acc)
    conv_ref[0:head, :] = conv_ref[tm:tm + head, :]
    dt_ref[...] = _softplus(_dot(h, wdt_ref[...]) + dtb_ref[...])

    for c in range(tm // SSD_CHUNK):
        rows = slice(c * SSD_CHUNK, (c + 1) * SSD_CHUNK)
        decays = _ssd_chunk_decays(rows, dt_ref, alog_ref, tri_ref, expand_ref)
        for g in range(n_groups):
            _ssd_chunk_group(g, rows, decays, xbc_ref, y_ref, state_ref, dskip_ref, d_inner=d_inner)

    gated = y_ref[...] * _silu(_dot(h_ref[...], win_ref[:, 0:d_inner]))
    parts = []
    for g in range(n_groups):
        gg = gated[:, g * group_w:(g + 1) * group_w]
        parts.append(gg * lax.rsqrt(jnp.mean(gg * gg, axis=-1, keepdims=True) + RMS_EPS))
    normed = (jnp.concatenate(parts, axis=1) * gnw_ref[...]).astype(BF16)
    o_ref[...] = x + _dot(normed, wout_ref[...].astype(BF16))


def _ssd(x, nw, w_in, w_dt, conv_w, conv_b, dt_bias, a_log, d_skip, norm_w, w_out, layer):
    s, d = x.shape
    d_inner = w_out.shape[-2]
    conv_dim = conv_w.shape[-1]
    n_heads = dt_bias.shape[0]
    hp = V7X_LANES
    assert n_heads <= hp and n_heads * SSD_HEAD_DIM == d_inner
    tm = SSD_ROW_TILE
    assert s % tm == 0 and tm % SSD_CHUNK == 0 and conv_dim % SSD_COL_CHUNK == 0

    def pad_heads(v):
        return jnp.pad(v, [(0, 0)] * (v.ndim - 1) + [(0, hp - n_heads)])

    wdt = pad_heads(w_dt).astype(BF16)
    dtb = pad_heads(dt_bias[None, :])
    alog = pad_heads(a_log[None, :])
    dskip = jnp.repeat(d_skip, SSD_HEAD_DIM)[None, :]
    tri = jnp.tile(jnp.tril(jnp.ones((SSD_CHUNK, SSD_CHUNK), BF16)), (1, SPLIT_TERMS))
    expand = jnp.tile((jnp.arange(hp)[:, None] == (jnp.arange(d_inner) // SSD_HEAD_DIM)[None, :]).astype(BF16),
                      (2, 1))

    scratch = [
        pltpu.VMEM((tm, d), BF16),
        pltpu.VMEM((tm + V7X_SUBLANES, conv_dim), F32),
        pltpu.VMEM((tm, conv_dim), F32),
        pltpu.VMEM((tm, hp), F32),
        pltpu.VMEM((tm, d_inner), F32),
        pltpu.VMEM((SSD_D_STATE, d_inner), F32),
    ]
    resident = (4 * _nbytes((tm, d), F32) + _nbytes(w_in.shape[1:], BF16) + _nbytes(w_out.shape[1:], F32)
                + _nbytes((d, hp), BF16) + _nbytes((2 * hp, d_inner), BF16) + _nbytes((tm, d), BF16)
                + _nbytes((2 * tm + V7X_SUBLANES, conv_dim), F32) + _nbytes((tm, hp), F32)
                + _nbytes((tm + SSD_D_STATE, d_inner), F32))
    return pl.pallas_call(
        functools.partial(_ssd_kernel, col_chunk=SSD_COL_CHUNK),
        out_shape=jax.ShapeDtypeStruct((s, d), F32),
        grid=(s // tm,),
        in_specs=[_row_tile(tm, d), _resident((1, d)), _resident_layer(w_in, layer),
                  _resident((d, hp)), _resident_layer(conv_w, layer), _resident((1, conv_dim)),
                  _resident((1, hp)), _resident((1, hp)), _resident((1, d_inner)), _resident((1, d_inner)),
                  _resident_layer(w_out, layer), _resident(tri.shape), _resident(expand.shape)],
        out_specs=_row_tile(tm, d),
        scratch_shapes=scratch,
        compiler_params=pltpu.CompilerParams(dimension_semantics=("arbitrary",),
                                             vmem_limit_bytes=_vmem_limit(resident)),
        name="ssd",
    )(x, nw, w_in, wdt, conv_w, conv_b[None, :], dtb, alog, dskip, norm_w[None, :], w_out, tri, expand)


def kernel(x, norm_w, ffn_w_gate, ffn_w_up, ffn_w_down, ssd_w_in, ssd_conv_w, ssd_conv_b, ssd_dt_bias,
           ssd_a_log, ssd_d, ssd_norm_w, ssd_w_out, sc_w_in, sc_conv_w, sc_w_out, final_norm_w):
    b, s, d = x.shape
    depth = norm_w.shape[0]
    xs = x.reshape(b * s, d)
    assert b == 1, "causal state is carried across row tiles of one sequence"
    fnw = final_norm_w[None, :]
    ssd_w_in_bf16 = ssd_w_in.astype(BF16)
    for i in range(depth):
        j = i // N_MIXERS
        xs = _ffn(xs, norm_w[i, 0][None, :], ffn_w_gate, ffn_w_up, ffn_w_down, fnw, (i, 0), final_norm=False)
        if i % N_MIXERS == 0:
            w_dt = ssd_w_in[j][:, -ssd_dt_bias.shape[1]:]
            xs = _ssd(xs, norm_w[i, 1][None, :], ssd_w_in_bf16, w_dt, ssd_conv_w, ssd_conv_b[j], ssd_dt_bias[j],
                      ssd_a_log[j], ssd_d[j], ssd_norm_w[j], ssd_w_out, (j,))
        else:
            xs = _shortconv(xs, norm_w[i, 1][None, :], sc_w_in, sc_conv_w, sc_w_out, (j,))
        xs = _ffn(xs, norm_w[i, 2][None, :], ffn_w_gate, ffn_w_up, ffn_w_down, fnw, (i, 1),
                  final_norm=(i == depth - 1))
    return xs.reshape(b, s, d)
```

```python
import functools
import math

import jax
import jax.numpy as jnp
from jax import lax
from jax.experimental import pallas as pl
from jax.experimental.pallas import tpu as pltpu

F32 = jnp.float32
BF16 = jnp.bfloat16

RMS_EPS = 1e-5
SSD_HEAD_DIM = 64
SSD_N_GROUPS = 4
SSD_D_STATE = 128
SSD_CHUNK = 128
N_MIXERS = 2
LOG2_E = 1.4426950408889634

V7X_LANES = 128
V7X_SUBLANES = 8
V7X_VMEM_BYTES = 64 * 1024 * 1024

FFN_ROW_TILE = 512
FFN_COL_CHUNK = 256
SC_ROW_TILE = 1024
SSD_ROW_TILE = 512
SSD_COL_CHUNK = 512
SPLIT_TERMS = 3


def _vmem_limit(resident_bytes):
    return int(min(2 * resident_bytes, V7X_VMEM_BYTES - 8 * 1024 * 1024))


def _nbytes(shape, dtype):
    return math.prod(shape) * jnp.dtype(dtype).itemsize


def _rmsnorm(x, w):
    inv = lax.rsqrt(jnp.mean(x * x, axis=-1, keepdims=True) + RMS_EPS)
    return (x * inv) * w


def _silu(x):
    return x * jax.nn.sigmoid(x)


def _softplus(x):
    return jnp.maximum(x, 0.0) + jnp.log1p(jnp.exp(-jnp.abs(x)))


def _dot(a, b):
    return jnp.dot(a, b, preferred_element_type=F32)


def _bf16_terms(v, n):
    terms = []
    for _ in range(n - 1):
        t = v.astype(BF16)
        terms.append(t)
        v = v - t.astype(F32)
    terms.append(v.astype(BF16))
    return terms


def _causal_conv(ext, w_ref, cols, k_w):
    t, c = ext.shape
    sub = V7X_SUBLANES
    last_sublane = lax.broadcasted_iota(jnp.int32, (t // sub, sub, c), 1) == sub - 1

    def shift_one_row(a):
        a = a.reshape(t // sub, sub, c)
        return pltpu.roll(jnp.where(last_sublane, pltpu.roll(a, 1, axis=0), a), 1, axis=1).reshape(t, c)

    acc = ext * w_ref[0:1, cols]
    for k in range(1, k_w):
        acc = ext * w_ref[k:k + 1, cols] + shift_one_row(acc)
    return acc


def _resident(shape):
    return pl.BlockSpec(shape, lambda i: (0,) * len(shape), pipeline_mode=pl.Buffered(1))


def _resident_layer(stacked, index):
    lead = len(index)
    shape = (None,) * lead + tuple(stacked.shape[lead:])
    return pl.BlockSpec(shape, lambda i: tuple(index) + (0,) * (stacked.ndim - lead),
                        pipeline_mode=pl.Buffered(1))


def _row_tile(tm, d):
    return pl.BlockSpec((tm, d), lambda i: (i, 0))


def _ffn_kernel(x_hbm, nw_ref, wg_hbm, wu_hbm, wd_hbm, fnw_ref, o_hbm,
                xbuf, obuf, wg_ref, wu_ref, wd_ref, act_ref, wsem, xsem, osem, *, layer, col_chunk, final_norm):
    tm = xbuf.shape[1]
    n_tiles = x_hbm.shape[0] // tm
    d_ff = wg_ref.shape[1]
    n_chunks = d_ff // col_chunk

    def chunk_copies(c):
        cols = pl.ds(c * col_chunk, col_chunk)
        return (pltpu.make_async_copy(wg_hbm.at[layer + (slice(None), cols)], wg_ref.at[:, cols], wsem.at[0, c]),
                pltpu.make_async_copy(wu_hbm.at[layer + (slice(None), cols)], wu_ref.at[:, cols], wsem.at[1, c]),
                pltpu.make_async_copy(wd_hbm.at[layer + (cols, slice(None))], wd_ref.at[cols, :], wsem.at[2, c]))

    def x_copy(i, slot):
        return pltpu.make_async_copy(x_hbm.at[pl.ds(i * tm, tm)], xbuf.at[slot], xsem.at[slot])

    def o_copy(i, slot):
        return pltpu.make_async_copy(obuf.at[slot], o_hbm.at[pl.ds(i * tm, tm)], osem.at[slot])

    def tile(slot, wait_for_weights):
        x = xbuf[slot]
        xw = (x * nw_ref[...]).astype(BF16)
        inv = lax.rsqrt(jnp.mean(x * x, axis=-1, keepdims=True) + RMS_EPS)
        for c in range(n_chunks):
            cols = slice(c * col_chunk, (c + 1) * col_chunk)
            if wait_for_weights:
                gate_copy, up_copy, _ = chunk_copies(c)
                gate_copy.wait()
                up_copy.wait()
            g = _dot(xw, wg_ref[:, cols].astype(BF16)) * inv
            u = _dot(xw, wu_ref[:, cols].astype(BF16)) * inv
            act_ref[:, cols] = (_silu(g) * u).astype(BF16)
        if wait_for_weights:
            for c in range(n_chunks):
                chunk_copies(c)[2].wait()
        out = x + 0.5 * _dot(act_ref[...], wd_ref[...].astype(BF16))
        if final_norm:
            out = _rmsnorm(out, fnw_ref[...])
        obuf[slot] = out

    x_copy(0, 0).start()
    for c in range(n_chunks):
        gate_copy, up_copy, _ = chunk_copies(c)
        gate_copy.start()
        up_copy.start()
    for c in range(n_chunks):
        chunk_copies(c)[2].start()
    x_copy(0, 0).wait()
    x_copy(1, 1).start()
    tile(0, True)
    o_copy(0, 0).start()

    def step(i, carry):
        slot = i % 2
        x_copy(i, slot).wait()
        x_copy(jnp.minimum(i + 1, n_tiles - 1), 1 - slot).start()

        @pl.when(i >= 2)
        def _():
            o_copy(i - 2, slot).wait()

        tile(slot, False)
        o_copy(i, slot).start()
        return carry

    lax.fori_loop(1, n_tiles, step, 0)
    x_copy(n_tiles - 1, n_tiles % 2).wait()
    o_copy(n_tiles - 2, n_tiles % 2).wait()
    o_copy(n_tiles - 1, (n_tiles - 1) % 2).wait()


def _ffn(x, nw, wg, wu, wd, fnw, layer, *, final_norm):
    s, d = x.shape
    d_ff = wg.shape[-1]
    tm = FFN_ROW_TILE
    assert s % tm == 0 and s // tm >= 2 and d_ff % FFN_COL_CHUNK == 0
    resident = (4 * _nbytes((tm, d), F32) + 3 * _nbytes((d, d_ff), F32) + _nbytes((tm, d_ff), BF16))
    in_hbm = pl.BlockSpec(memory_space=pl.ANY)
    in_vmem = pl.BlockSpec(memory_space=pltpu.VMEM)
    return pl.pallas_call(
        functools.partial(_ffn_kernel, layer=tuple(layer), col_chunk=FFN_COL_CHUNK, final_norm=final_norm),
        out_shape=jax.ShapeDtypeStruct((s, d), F32),
        in_specs=[in_hbm, in_vmem, in_hbm, in_hbm, in_hbm, in_vmem],
        out_specs=in_hbm,
        scratch_shapes=[pltpu.VMEM((2, tm, d), F32), pltpu.VMEM((2, tm, d), F32),
                        pltpu.VMEM((d, d_ff), F32), pltpu.VMEM((d, d_ff), F32), pltpu.VMEM((d_ff, d), F32),
                        pltpu.VMEM((tm, d_ff), BF16), pltpu.SemaphoreType.DMA((3, d_ff // FFN_COL_CHUNK)),
                        pltpu.SemaphoreType.DMA((2,)), pltpu.SemaphoreType.DMA((2,))],
        compiler_params=pltpu.CompilerParams(vmem_limit_bytes=_vmem_limit(resident)),
        name="ffn",
    )(x, nw, wg, wu, wd, fnw)


def _sc_kernel(x_ref, nw_ref, win_ref, cw_ref, wout_ref, o_ref, cu_ref):
    tm, d = x_ref.shape
    k_w = cw_ref.shape[0]
    head = V7X_SUBLANES

    @pl.when(pl.program_id(0) == 0)
    def _():
        cu_ref[0:head, :] = jnp.zeros((head, d), F32)

    x = x_ref[...]
    xw = (x * nw_ref[...]).astype(BF16)
    inv = lax.rsqrt(jnp.mean(x * x, axis=-1, keepdims=True) + RMS_EPS)
    cg = _dot(xw, win_ref[:, d:2 * d].astype(BF16))
    u = _dot(xw, win_ref[:, 2 * d:3 * d].astype(BF16))
    cu_ref[head:head + tm, :] = (cg * u) * (inv * inv)
    v = _causal_conv(cu_ref[...], cw_ref, slice(0, d), k_w)[head:head + tm]
    cu_ref[0:head, :] = cu_ref[tm:tm + head, :]
    bg = _dot(xw, win_ref[:, 0:d].astype(BF16)) * inv
    o_ref[...] = x + _dot((bg * v).astype(BF16), wout_ref[...].astype(BF16))


def _shortconv(x, nw, win, cw, wout, layer):
    s, d = x.shape
    tm = SC_ROW_TILE
    assert s % tm == 0
    resident = (4 * _nbytes((tm, d), F32) + _nbytes((d, 3 * d), F32) + _nbytes((d, d), F32)
                + _nbytes((tm + V7X_SUBLANES, d), F32))
    return pl.pallas_call(
        _sc_kernel,
        out_shape=jax.ShapeDtypeStruct((s, d), F32),
        grid=(s // tm,),
        in_specs=[_row_tile(tm, d), _resident((1, d)), _resident_layer(win, layer), _resident_layer(cw, layer),
                  _resident_layer(wout, layer)],
        out_specs=_row_tile(tm, d),
        scratch_shapes=[pltpu.VMEM((tm + V7X_SUBLANES, d), F32)],
        compiler_params=pltpu.CompilerParams(dimension_semantics=("arbitrary",),
                                             vmem_limit_bytes=_vmem_limit(resident)),
        name="shortconv",
    )(x, nw, win, cw, wout)


def _ssd_chunk_decays(rows, dt_ref, alog_ref, tri_ref, expand_ref):
    chunk = SSD_CHUNK
    dt = dt_ref[rows, :]
    ac = dt * (-jnp.exp(alog_ref[...]))
    a_cs = _dot(tri_ref[...], jnp.concatenate(_bf16_terms(ac, SPLIT_TERMS), axis=0))
    a2 = a_cs * LOG2_E
    s_side_t = a2.T - jnp.log2(dt.T)
    tot = a_cs[chunk - 1:chunk, :]
    per_head = jnp.concatenate(
        [jnp.exp(tot - a_cs) * dt, jnp.exp(a_cs), jnp.broadcast_to(jnp.exp(tot), (V7X_SUBLANES, tot.shape[1]))],
        axis=0)
    per_col = _dot(jnp.concatenate(_bf16_terms(per_head, 2), axis=1), expand_ref[...])
    return a2, s_side_t, per_col[0:chunk], per_col[chunk:2 * chunk], per_col[2 * chunk:2 * chunk + 1]


def _ssd_chunk_group(g, rows, decays, xbc_ref, y_ref, state_ref, dskip_ref, *, d_inner):
    a2, s_side_t, w_state, w_off, w_tot = decays
    chunk = SSD_CHUNK
    n_state = state_ref.shape[0]
    n_groups = (xbc_ref.shape[1] - d_inner) // (2 * n_state)
    group_w = d_inner // n_groups
    pair_w = 2 * SSD_HEAD_DIM
    causal = (lax.broadcasted_iota(jnp.int32, (chunk, chunk), 0)
              >= lax.broadcasted_iota(jnp.int32, (chunk, chunk), 1))
    first_head = lax.broadcasted_iota(jnp.int32, (chunk, pair_w), 1) < SSD_HEAD_DIM

    gcols = slice(g * group_w, (g + 1) * group_w)
    xg = xbc_ref[rows, gcols]
    bg = xbc_ref[rows, d_inner + g * n_state:d_inner + (g + 1) * n_state].astype(BF16)
    cg = xbc_ref[rows, d_inner + (n_groups + g) * n_state:d_inner + (n_groups + g + 1) * n_state].astype(BF16)
    cb = lax.dot_general(cg, bg, (((1,), (1,)), ((), ())), preferred_element_type=F32)
    s_old = state_ref[:, gcols]
    y_off = _dot(cg, s_old.astype(BF16)) * w_off[:, gcols]
    wx = (w_state[:, gcols] * xg).astype(BF16)
    state_ref[:, gcols] = w_tot[:, gcols] * s_old + lax.dot_general(
        bg, wx, (((0,), (0,)), ((), ())), preferred_element_type=F32)
    for j in range(group_w // pair_w):
        h0 = (g * group_w + j * pair_w) // SSD_HEAD_DIM
        pcols = slice(g * group_w + j * pair_w, g * group_w + (j + 1) * pair_w)
        m = []
        for hh in (h0, h0 + 1):
            seg2 = a2[:, hh:hh + 1] - s_side_t[hh:hh + 1, :]
            m.append(cb * jnp.exp2(jnp.where(causal, seg2, -jnp.inf)))
        lhs = jnp.concatenate(m, axis=1).astype(BF16)
        xp = xg[:, j * pair_w:(j + 1) * pair_w]
        rhs = jnp.concatenate([jnp.where(first_head, xp, 0.0), jnp.where(first_head, 0.0, xp)],
                              axis=0).astype(BF16)
        y_ref[rows, pcols] = (_dot(lhs, rhs) + y_off[:, j * pair_w:(j + 1) * pair_w]
                              + xp * dskip_ref[:, pcols])


def _ssd_kernel(x_ref, nw_ref, win_ref, wdt_ref, cw_ref, cb_ref, dtb_ref, alog_ref, dskip_ref, gnw_ref,
                wout_ref, tri_ref, expand_ref,
                o_ref,
                h_ref, conv_ref, xbc_ref, dt_ref, y_ref, state_ref, *, col_chunk):
    tm, d = x_ref.shape
    d_inner = wout_ref.shape[0]
    conv_dim = cw_ref.shape[1]
    k_w = cw_ref.shape[0]
    n_groups = (conv_dim - d_inner) // (2 * state_ref.shape[0])
    group_w = d_inner // n_groups
    head = V7X_SUBLANES

    @pl.when(pl.program_id(0) == 0)
    def _():
        conv_ref[0:head, :] = jnp.zeros((head, conv_dim), F32)
        state_ref[...] = jnp.zeros(state_ref.shape, F32)

    x = x_ref[...]
    h = _rmsnorm(x, nw_ref[...]).astype(BF16)
    h_ref[...] = h
    for c in range(conv_dim // col_chunk):
        cols = slice(c * col_chunk, (c + 1) * col_chunk)
        conv_ref[head:head + tm, cols] = _dot(
            h, win_ref[:, d_inner + c * col_chunk:d_inner + (c + 1) * col_chunk])
        acc = _causal_conv(conv_ref[:, cols], cw_ref, cols, k_w)[head:head + tm] + cb_ref[:, cols]
        xbc_ref[:, cols] = _silu(acc)
    conv_ref[0:head, :] = conv_ref[tm:tm + head, :]
    dt_ref[...] = _softplus(_dot(h, wdt_ref[...]) + dtb_ref[...])

    for c in range(tm // SSD_CHUNK):
        rows = slice(c * SSD_CHUNK, (c + 1) * SSD_CHUNK)
        decays = _ssd_chunk_decays(rows, dt_ref, alog_ref, tri_ref, expand_ref)
        for g in range(n_groups):
            _ssd_chunk_group(g, rows, decays, xbc_ref, y_ref, state_ref, dskip_ref, d_inner=d_inner)

    gated = y_ref[...] * _silu(_dot(h_ref[...], win_ref[:, 0:d_inner]))
    parts = []
    for g in range(n_groups):
        gg = gated[:, g * group_w:(g + 1) * group_w]
        parts.append(gg * lax.rsqrt(jnp.mean(gg * gg, axis=-1, keepdims=True) + RMS_EPS))
    normed = (jnp.concatenate(parts, axis=1) * gnw_ref[...]).astype(BF16)
    o_ref[...] = x + _dot(normed, wout_ref[...].astype(BF16))


def _ssd(x, nw, w_in, conv_w, conv_b, dt_bias, a_log, d_skip, norm_w, w_out, layer):
    s, d = x.shape
    d_inner = w_out.shape[-2]
    conv_dim = conv_w.shape[-1]
    n_heads = dt_bias.shape[0]
    hp = V7X_LANES
    assert n_heads <= hp and n_heads * SSD_HEAD_DIM == d_inner
    tm = SSD_ROW_TILE
    assert s % tm == 0 and tm % SSD_CHUNK == 0 and conv_dim % SSD_COL_CHUNK == 0

    def pad_heads(v):
        return jnp.pad(v, [(0, 0)] * (v.ndim - 1) + [(0, hp - n_heads)])

    wdt = pad_heads(w_in[layer][:, d_inner + conv_dim:])
    dtb = pad_heads(dt_bias[None, :])
    alog = pad_heads(a_log[None, :])
    dskip = jnp.repeat(d_skip, SSD_HEAD_DIM)[None, :]
    tri = jnp.tile(jnp.tril(jnp.ones((SSD_CHUNK, SSD_CHUNK), BF16)), (1, SPLIT_TERMS))
    expand = jnp.tile((jnp.arange(hp)[:, None] == (jnp.arange(d_inner) // SSD_HEAD_DIM)[None, :]).astype(BF16),
                      (2, 1))

    scratch = [
        pltpu.VMEM((tm, d), BF16),
        pltpu.VMEM((tm + V7X_SUBLANES, conv_dim), F32),
        pltpu.VMEM((tm, conv_dim), F32),
        pltpu.VMEM((tm, hp), F32),
        pltpu.VMEM((tm, d_inner), F32),
        pltpu.VMEM((SSD_D_STATE, d_inner), F32),
    ]
    resident = (4 * _nbytes((tm, d), F32) + _nbytes(w_in.shape[1:], BF16) + _nbytes(w_out.shape[1:], F32)
                + _nbytes((d, hp), BF16) + _nbytes((2 * hp, d_inner), BF16) + _nbytes((tm, d), BF16)
                + _nbytes((2 * tm + V7X_SUBLANES, conv_dim), F32) + _nbytes((tm, hp), F32)
                + _nbytes((tm + SSD_D_STATE, d_inner), F32))
    return pl.pallas_call(
        functools.partial(_ssd_kernel, col_chunk=SSD_COL_CHUNK),
        out_shape=jax.ShapeDtypeStruct((s, d), F32),
        grid=(s // tm,),
        in_specs=[_row_tile(tm, d), _resident((1, d)), _resident_layer(w_in, layer),
                  _resident((d, hp)), _resident_layer(conv_w, layer), _resident((1, conv_dim)),
                  _resident((1, hp)), _resident((1, hp)), _resident((1, d_inner)), _resident((1, d_inner)),
                  _resident_layer(w_out, layer), _resident(tri.shape), _resident(expand.shape)],
        out_specs=_row_tile(tm, d),
        scratch_shapes=scratch,
        compiler_params=pltpu.CompilerParams(dimension_semantics=("arbitrary",),
                                             vmem_limit_bytes=_vmem_limit(resident)),
        name="ssd",
    )(x, nw, w_in, wdt, conv_w, conv_b[None, :], dtb, alog, dskip, norm_w[None, :], w_out, tri, expand)


def kernel(x, norm_w, ffn_w_gate, ffn_w_up, ffn_w_down, ssd_w_in, ssd_conv_w, ssd_conv_b, ssd_dt_bias,
           ssd_a_log, ssd_d, ssd_norm_w, ssd_w_out, sc_w_in, sc_conv_w, sc_w_out, final_norm_w):
    b, s, d = x.shape
    depth = norm_w.shape[0]
    xs = x.reshape(b * s, d)
    assert b == 1, "causal state is carried across row tiles of one sequence"
    fnw = final_norm_w[None, :]
    ssd_w_in_bf16 = ssd_w_in.astype(BF16)
    for i in range(depth):
        j = i // N_MIXERS
        xs = _ffn(xs, norm_w[i, 0][None, :], ffn_w_gate, ffn_w_up, ffn_w_down, fnw, (i, 0), final_norm=False)
        if i % N_MIXERS == 0:
            xs = _ssd(xs, norm_w[i, 1][None, :], ssd_w_in_bf16, ssd_conv_w, ssd_conv_b[j], ssd_dt_bias[j],
                      ssd_a_log[j], ssd_d[j], ssd_norm_w[j], ssd_w_out, (j,))
        else:
            xs = _shortconv(xs, norm_w[i, 1][None, :], sc_w_in, sc_conv_w, sc_w_out, (j,))
        xs = _ffn(xs, norm_w[i, 2][None, :], ffn_w_gate, ffn_w_up, ffn_w_down, fnw, (i, 1),
                  final_norm=(i == depth - 1))
    return xs.reshape(b, s, d)
```
